```python
import math
import jax, jax.numpy as jnp
from jax import lax
import numpy as np

D_MODEL = 1024
BATCH = 8
SEQ = 8192
DEPTH = 1
DEC_BATCH = 8
DEC_SEQ = 64
PAST_LEN = 1024

CHUNK = 64
D_MIX = 2 * D_MODEL
D_CONV = D_MIX // 2
CONV_GROUPS = 16
CONV_A_WIDTH = 3
D_SSM = D_MIX - D_CONV
SSM_HEAD_DIM = 64
SSM_HEADS = D_SSM // SSM_HEAD_DIM
SSM_GROUPS = 2
SSM_STATE = 128
SSM_CONV_WIDTH = 4
D_XBC = D_SSM + 2 * SSM_GROUPS * SSM_STATE
D_IN_PROJ = 3 * D_CONV + D_SSM + D_XBC + SSM_HEADS
N_EXPERTS = 32
TOP_K = 4
D_EXPERT = D_MODEL
SWIGLU_LIMIT = 7.0
SWIGLU_ALPHA = 1.702
MOE_BLOCK = 256
EPS = 1e-5

kernel_name = 'hybrid_shortconv_ssd_moe_stream_step'


def _rmsnorm(x, g):
    xf = x.astype(jnp.float32)
    y = xf * lax.rsqrt(jnp.mean(xf * xf, axis=-1, keepdims=True) + EPS)
    return (y * g.astype(jnp.float32)).astype(x.dtype)


def _group_rmsnorm(y, n_groups, g):
    shape = y.shape
    yf = y.astype(jnp.float32).reshape(shape[:-1] + (n_groups, shape[-1] // n_groups))
    yf = yf * lax.rsqrt(jnp.mean(yf * yf, axis=-1, keepdims=True) + EPS)
    return yf.reshape(shape) * g.astype(jnp.float32)


def _causal_dwconv(xp, w, length):
    out = xp[:, 0:length] * w[0]
    for k in range(1, w.shape[0]):
        out = out + xp[:, k:k + length] * w[k]
    return out


def _ssd_scan(x, dt, a, bm, cm, h0, block):
    b, l, nh, p = x.shape
    g, n = bm.shape[2], bm.shape[3]
    r = nh // g
    nc = l // block
    xdt = (x * dt[..., None]).reshape(b, nc, block, g, r, p)
    bc = bm.reshape(b, nc, block, g, n)
    cc = cm.reshape(b, nc, block, g, n)
    da = (dt * a).reshape(b, nc, block, g, r).transpose(0, 3, 4, 1, 2)
    a_cum = jnp.cumsum(da, axis=-1)
    causal = jnp.tril(jnp.ones((block, block), dtype=bool))
    lmat = jnp.exp(jnp.where(causal, a_cum[..., :, None] - a_cum[..., None, :], -jnp.inf))
    cb = jnp.einsum('bclgn,bcsgn->bgcls', cc, bc)
    y_diag = jnp.einsum('bgcls,bgrcls,bcsgrp->bclgrp', cb, lmat, xdt)
    decay = jnp.exp(a_cum[..., -1:] - a_cum)
    states = jnp.einsum('bcsgn,bgrcs,bcsgrp->cbgrpn', bc, decay, xdt)
    chunk_decay = jnp.exp(a_cum[..., -1]).transpose(3, 0, 1, 2)

    def step(carry, inp):
        s, d = inp
        return carry * d[..., None, None] + s, carry

    h_last, h_in = lax.scan(step, h0.reshape(b, g, r, p, n), (states, chunk_decay))
    y_off = jnp.einsum('bclgn,cbgrpn,bgrcl->bclgrp', cc, h_in, jnp.exp(a_cum))
    y = (y_diag + y_off).reshape(b, l, nh, p)
    return y, h_last.reshape(b, nh, p, n)


def _moe(xt, w_router, b_router, w_gate, b_gate, w_up, b_up, w_down, b_down):
    t, d = xt.shape
    logits = (xt @ w_router + b_router).astype(jnp.float32)
    top_v, top_i = lax.top_k(logits, TOP_K)
    gates = jax.nn.softmax(top_v, axis=-1)
    flat_e = top_i.reshape(-1)
    flat_tok = jnp.repeat(jnp.arange(t, dtype=jnp.int32), TOP_K)
    flat_gate = gates.reshape(-1)
    order = jnp.argsort(flat_e)
    sorted_e = flat_e[order]
    counts = jnp.bincount(flat_e, length=N_EXPERTS)
    padded = (counts + MOE_BLOCK - 1) // MOE_BLOCK * MOE_BLOCK
    padded_end = jnp.cumsum(padded)
    padded_start = padded_end - padded
    group_start = jnp.cumsum(counts) - counts
    rank = jnp.arange(t * TOP_K, dtype=jnp.int32) - group_start[sorted_e]
    dest = padded_start[sorted_e] + rank
    n_blocks = -(-(t * TOP_K) // MOE_BLOCK) + N_EXPERTS
    n_slots = n_blocks * MOE_BLOCK
    slot_tok = jnp.full((n_slots,), t, dtype=jnp.int32).at[dest].set(flat_tok[order])
    slot_gate = jnp.zeros((n_slots,), jnp.float32).at[dest].set(flat_gate[order])
    block_start = jnp.arange(n_blocks, dtype=jnp.int32) * MOE_BLOCK
    block_expert = jnp.minimum(jnp.searchsorted(padded_end, block_start, side='right'), N_EXPERTS - 1)
    x_pad = jnp.concatenate([xt, jnp.zeros((1, d), xt.dtype)], axis=0)
    xb = x_pad[slot_tok].reshape(n_blocks, MOE_BLOCK, d)

    def run_block(args):
        xblk, e = args
        gate = xblk @ w_gate[e] + b_gate[e]
        up = xblk @ w_up[e] + b_up[e]
        gate = jnp.minimum(gate, SWIGLU_LIMIT)
        up = jnp.clip(up, -SWIGLU_LIMIT, SWIGLU_LIMIT)
        glu = gate * jax.nn.sigmoid(gate * SWIGLU_ALPHA)
        return ((up + 1.0) * glu) @ w_down[e] + b_down[e]

    yb = lax.map(run_block, (xb, block_expert)).reshape(n_slots, d)
    y = jnp.zeros((t + 1, d), jnp.float32).at[slot_tok].add(yb.astype(jnp.float32) * slot_gate[:, None])
    return y[:t].astype(xt.dtype)


def _layer(x, conv_a_state, conv_s_state, ssm_state, p):
    b, l, _ = x.shape
    xn = _rmsnorm(x, p['norm_mix'])
    proj = xn @ p['w_in']
    cuts = [D_CONV, 2 * D_CONV, 3 * D_CONV, 3 * D_CONV + D_SSM, 3 * D_CONV + D_SSM + D_XBC]
    a_b, a_c, a_h, z, xbc, dt_raw = jnp.split(proj, cuts, axis=-1)
    u = a_c * a_h
    u_pad = jnp.concatenate([conv_a_state.astype(u.dtype), u], axis=1)
    y_a = a_b * _causal_dwconv(u_pad, p['conv_a_w'], l)
    new_conv_a = u_pad[:, -(CONV_A_WIDTH - 1):]
    y_a = _group_rmsnorm(y_a, CONV_GROUPS, p['norm_conv_a'])
    xbc_pad = jnp.concatenate([conv_s_state.astype(xbc.dtype), xbc], axis=1)
    xbc_c = jax.nn.silu(_causal_dwconv(xbc_pad, p['conv_ssm_w'], l) + p['conv_ssm_b'])
    new_conv_s = xbc_pad[:, -(SSM_CONV_WIDTH - 1):]
    xs, bs, cs = jnp.split(xbc_c, [D_SSM, D_SSM + SSM_GROUPS * SSM_STATE], axis=-1)
    xs_h = xs.reshape(b, l, SSM_HEADS, SSM_HEAD_DIM).astype(jnp.float32)
    dt = jax.nn.softplus(dt_raw.astype(jnp.float32) + p['dt_bias'].astype(jnp.float32))
    a = -jnp.exp(p['a_log'].astype(jnp.float32))
    block = CHUNK if l % CHUNK == 0 else l
    y_s, new_ssm = _ssd_scan(xs_h, dt, a,
                             bs.reshape(b, l, SSM_GROUPS, SSM_STATE).astype(jnp.float32),
                             cs.reshape(b, l, SSM_GROUPS, SSM_STATE).astype(jnp.float32),
                             ssm_state.astype(jnp.float32), block)
    y_s = (y_s + p['d_skip'].astype(jnp.float32)[:, None] * xs_h).reshape(b, l, D_SSM)
    y_s = _group_rmsnorm(y_s * jax.nn.silu(z.astype(jnp.float32)), SSM_GROUPS, p['norm_ssm'])
    mixed = jnp.concatenate([y_a, y_s], axis=-1).astype(x.dtype)
    x = x + mixed @ p['w_out']
    xf = _rmsnorm(x, p['norm_ffn']).reshape(b * l, D_MODEL)
    x = x + _moe(xf, p['w_router'], p['b_router'], p['w_gate'], p['b_gate'],
                 p['w_up'], p['b_up'], p['w_down'], p['b_down']).reshape(b, l, D_MODEL)
    return x, new_conv_a, new_conv_s, new_ssm.astype(x.dtype)


def setup_inputs(seed: int = 0) -> dict:
    key = jax.random.key(seed)
    ks = jax.random.split(key, 26)

    def nrm(k, shape, scale):
        return scale * jax.random.normal(k, shape, jnp.float32)

    dt0 = jnp.exp(jax.random.uniform(ks[11], (DEPTH, SSM_HEADS), jnp.float32)
                  * (math.log(0.1) - math.log(1e-3)) + math.log(1e-3))
    dt_bias = dt0 + jnp.log(-jnp.expm1(-dt0))
    return {
        'x_prompt': nrm(ks[0], (BATCH, SEQ, D_MODEL), 1.0),
        'x_sample': nrm(ks[1], (DEC_BATCH, DEC_SEQ, D_MODEL), 1.0),
        'cache_conv_a': nrm(ks[2], (DEPTH, DEC_BATCH, CONV_A_WIDTH - 1, D_CONV), 1.0),
        'cache_conv_ssm': nrm(ks[3], (DEPTH, DEC_BATCH, SSM_CONV_WIDTH - 1, D_XBC), 1.0),
        'state_ssm': nrm(ks[4], (DEPTH, DEC_BATCH, SSM_HEADS, SSM_HEAD_DIM, SSM_STATE), 0.2),
        'norm_mix': 1.0 + nrm(ks[5], (DEPTH, D_MODEL), 0.02),
        'w_in': nrm(ks[6], (DEPTH, D_MODEL, D_IN_PROJ), D_MODEL ** -0.5),
        'conv_a_w': nrm(ks[7], (DEPTH, CONV_A_WIDTH, D_CONV), CONV_A_WIDTH ** -0.5),
        'norm_conv_a': 1.0 + nrm(ks[8], (DEPTH, D_CONV), 0.02),
        'conv_ssm_w': nrm(ks[9], (DEPTH, SSM_CONV_WIDTH, D_XBC), SSM_CONV_WIDTH ** -0.5),
        'conv_ssm_b': nrm(ks[10], (DEPTH, D_XBC), 0.02),
        'dt_bias': dt_bias,
        'a_log': jnp.log(jax.random.uniform(ks[12], (DEPTH, SSM_HEADS), jnp.float32, minval=1.0, maxval=16.0)),
        'd_skip': 1.0 + nrm(ks[13], (DEPTH, SSM_HEADS), 0.1),
        'norm_ssm': 1.0 + nrm(ks[14], (DEPTH, D_SSM), 0.02),
        'w_out': nrm(ks[15], (DEPTH, D_MIX, D_MODEL), D_MIX ** -0.5),
        'norm_ffn': 1.0 + nrm(ks[16], (DEPTH, D_MODEL), 0.02),
        'w_router': nrm(ks[17], (DEPTH, D_MODEL, N_EXPERTS), D_MODEL ** -0.5),
        'b_router': nrm(ks[18], (DEPTH, N_EXPERTS), 0.01),
        'w_gate': nrm(ks[19], (DEPTH, N_EXPERTS, D_MODEL, D_EXPERT), D_MODEL ** -0.5),
        'b_gate': nrm(ks[20], (DEPTH, N_EXPERTS, D_EXPERT), 0.01),
        'w_up': nrm(ks[21], (DEPTH, N_EXPERTS, D_MODEL, D_EXPERT), D_MODEL ** -0.5),
        'b_up': nrm(ks[22], (DEPTH, N_EXPERTS, D_EXPERT), 0.01),
        'w_down': nrm(ks[23], (DEPTH, N_EXPERTS, D_EXPERT, D_MODEL), D_EXPERT ** -0.5),
        'b_down': nrm(ks[24], (DEPTH, N_EXPERTS, D_MODEL), 0.01),
        'final_norm': 1.0 + nrm(ks[25], (D_MODEL,), 0.02),
    }


def reference(x_prompt, x_sample, cache_conv_a, cache_conv_ssm, state_ssm,
              norm_mix, w_in, conv_a_w, norm_conv_a, conv_ssm_w, conv_ssm_b,
              dt_bias, a_log, d_skip, norm_ssm, w_out, norm_ffn,
              w_router, b_router, w_gate, b_gate, w_up, b_up, w_down, b_down,
              final_norm):
    bp = x_prompt.shape[0]
    dt_ = x_prompt.dtype
    h_p, h_s = x_prompt, x_sample
    ca_p, cs_p, st_p, ca_s, cs_s, st_s = [], [], [], [], [], []
    for i in range(DEPTH):
        p = {
            'norm_mix': norm_mix[i], 'w_in': w_in[i], 'conv_a_w': conv_a_w[i],
            'norm_conv_a': norm_conv_a[i], 'conv_ssm_w': conv_ssm_w[i], 'conv_ssm_b': conv_ssm_b[i],
            'dt_bias': dt_bias[i], 'a_log': a_log[i], 'd_skip': d_skip[i], 'norm_ssm': norm_ssm[i],
            'w_out': w_out[i], 'norm_ffn': norm_ffn[i], 'w_router': w_router[i], 'b_router': b_router[i],
            'w_gate': w_gate[i], 'b_gate': b_gate[i], 'w_up': w_up[i], 'b_up': b_up[i],
            'w_down': w_down[i], 'b_down': b_down[i],
        }
        h_p, a1, s1, t1 = _layer(h_p,
                                 jnp.zeros((bp, CONV_A_WIDTH - 1, D_CONV), dt_),
                                 jnp.zeros((bp, SSM_CONV_WIDTH - 1, D_XBC), dt_),
                                 jnp.zeros((bp, SSM_HEADS, SSM_HEAD_DIM, SSM_STATE), dt_), p)
        h_s, a2, s2, t2 = _layer(h_s, cache_conv_a[i], cache_conv_ssm[i], state_ssm[i], p)
        ca_p.append(a1); cs_p.append(s1); st_p.append(t1)
        ca_s.append(a2); cs_s.append(s2); st_s.append(t2)
    y_prompt = _rmsnorm(h_p, final_norm)
    y_sample = _rmsnorm(h_s, final_norm)
    return (y_prompt, y_sample,
            jnp.stack(ca_p, 0), jnp.stack(cs_p, 0), jnp.stack(st_p, 0),
            jnp.stack(ca_s, 0), jnp.stack(cs_s, 0), jnp.stack(st_s, 0))
```

```python
import functools

import jax
import jax.numpy as jnp
from jax import lax
from jax.experimental import pallas as pl
from jax.experimental.pallas import tpu as pltpu

F32 = jnp.float32
BF16 = jnp.bfloat16

D_MODEL = 1024
D_CONV = 1024
D_SSM = 1024
CONV_GROUPS = 16
CONV_A_WIDTH = 3
SSM_HEAD_DIM = 64
SSM_HEADS = 16
SSM_GROUPS = 2
SSM_STATE = 128
SSM_CONV_WIDTH = 4
D_XBC = D_SSM + 2 * SSM_GROUPS * SSM_STATE
N_EXPERTS = 32
TOP_K = 4
SWIGLU_LIMIT = 7.0
SWIGLU_ALPHA = 1.702
MOE_BLOCK = 256
EPS = 1e-5

LANES = 128
COL_A = 3 * D_CONV
COL_S = D_SSM + D_XBC + LANES
D_IN_PAD = COL_A + COL_S
NEG_BIG = -1e30
VMEM_LIMIT = 56 * 1024 * 1024


def _softplus(v):
    return jnp.maximum(v, 0.0) + jnp.log1p(jnp.exp(-jnp.abs(v)))


def _silu(v):
    return v * jax.nn.sigmoid(v)


def _split2(v):
    hi = v.astype(BF16)
    mid = (v - hi.astype(F32)).astype(BF16)
    return hi, mid


def _transpose_rows(v):
    r = v.shape[0]
    if r % LANES:
        v = jnp.concatenate([v, jnp.zeros((LANES - r % LANES, v.shape[1]), v.dtype)], axis=0)
    return v.T[:, :r]


def _ssd_chunk(xs, bm, cm, dt, a_neg, z, dskip, s_scr, tril, e2):
    q = xs.shape[0]
    da = dt * a_neg
    d_hi = da.astype(BF16)
    r1 = da - d_hi.astype(F32)
    d_mid = r1.astype(BF16)
    d_lo = (r1 - d_mid.astype(F32)).astype(BF16)
    cs = jnp.dot(tril, jnp.concatenate([d_hi, d_mid, d_lo], axis=1), preferred_element_type=F32)
    acum = cs[:, 0:LANES] + cs[:, LANES:2 * LANES] + cs[:, 2 * LANES:3 * LANES]
    acum_t = _transpose_rows(acum)
    total = acum[q - 1:q, :]
    stack = jnp.concatenate(
        [dt, jnp.exp(total - acum), jnp.exp(acum), jnp.broadcast_to(jnp.exp(total), (8, LANES))], axis=0)
    s_hi, s_mid = _split2(stack)
    ex = jnp.dot(jnp.concatenate([s_hi, s_mid], axis=1), e2, preferred_element_type=F32)
    dt_e, dec_e, ea_e, cd_e = ex[0:q], ex[q:2 * q], ex[2 * q:3 * q], ex[3 * q:3 * q + 1]
    xdt = xs * dt_e
    xdt_b = xdt.astype(BF16)
    xdec_b = (xdt * dec_e).astype(BF16)
    row = lax.broadcasted_iota(jnp.int32, (q, q), 0)
    col = lax.broadcasted_iota(jnp.int32, (q, q), 1)
    causal = row >= col
    lane = lax.broadcasted_iota(jnp.int32, (q, LANES), 1)
    gw = D_SSM // SSM_GROUPS
    hpg = SSM_HEADS // SSM_GROUPS
    ys = []
    for g in range(SSM_GROUPS):
        cg = cm[:, g * SSM_STATE:(g + 1) * SSM_STATE].astype(BF16)
        bg = bm[:, g * SSM_STATE:(g + 1) * SSM_STATE].astype(BF16)
        cb = lax.dot_general(cg, bg, (((1,), (1,)), ((), ())), preferred_element_type=F32)
        s_g = s_scr[g]
        y_off = jnp.dot(cg, s_g.astype(BF16), preferred_element_type=F32) * ea_e[:, g * gw:(g + 1) * gw]
        pieces = []
        for j in range(hpg // 2):
            h0 = g * hpg + 2 * j
            ms = []
            for h in (h0, h0 + 1):
                diff = acum[:, h:h + 1] - acum_t[h:h + 1, :]
                ms.append(cb * jnp.exp(jnp.where(causal, diff, NEG_BIG)))
            m = jnp.concatenate(ms, axis=1).astype(BF16)
            xp = xdt_b[:, h0 * SSM_HEAD_DIM:(h0 + 2) * SSM_HEAD_DIM]
            zero = jnp.zeros_like(xp)
            rhs = jnp.concatenate([jnp.where(lane < SSM_HEAD_DIM, xp, zero),
                                   jnp.where(lane >= SSM_HEAD_DIM, xp, zero)], axis=0)
            pieces.append(jnp.dot(m, rhs, preferred_element_type=F32))
        ys.append(jnp.concatenate(pieces, axis=1) + y_off)
        upd = lax.dot_general(bg, xdec_b[:, g * gw:(g + 1) * gw], (((0,), (0,)), ((), ())),
                              preferred_element_type=F32)
        s_scr[g] = s_g * cd_e[:, g * gw:(g + 1) * gw] + upd
    y = jnp.concatenate(ys, axis=1) + dskip * xs
    return y * _silu(z)


def _mixer_kernel(*refs, tl, q, nl, n_alias):
    (x_ref, ca0_ref, cs0_ref, ssm0_ref, nmix_ref, win_ref, caw_ref, na_ref, csw_ref, csb_ref,
     dtb_ref, alog_ref, dsk_ref, ns_ref, wout_ref, nffn_ref, wr_ref, br_ref,
     tril_ref, e2_ref, g_ref, e3_ref) = refs[:22]
    refs = refs[22 + n_alias:]
    x1_ref, xf_ref, rrow_ref, route_ref, ca_ref, cs_ref, ssm_ref, upad, xpad, s_scr, ys_scr = refs
    l = pl.program_id(1)
    last = nl - 1
    pa_rows = CONV_A_WIDTH - 1
    ps_rows = SSM_CONV_WIDTH - 1

    @pl.when(l == 0)
    def _():
        upad[8 - pa_rows:8, :] = ca0_ref[...]
        xpad[8 - ps_rows:8, :] = cs0_ref[...]
        s_scr[...] = ssm0_ref[...]

    x = x_ref[...]
    xn = (x * lax.rsqrt(jnp.mean(x * x, axis=-1, keepdims=True) + EPS) * nmix_ref[...]).astype(BF16)

    pa = jnp.dot(xn, win_ref[:, 0:COL_A], preferred_element_type=F32)
    a_b, a_c, a_h = pa[:, 0:D_CONV], pa[:, D_CONV:2 * D_CONV], pa[:, 2 * D_CONV:3 * D_CONV]
    upad[8:8 + tl, :] = a_c * a_h
    caw = caw_ref[...]
    conv = upad[8 - pa_rows:8 - pa_rows + tl, :] * caw[0:1, :]
    for k in range(1, CONV_A_WIDTH):
        conv = conv + upad[8 - pa_rows + k:8 - pa_rows + k + tl, :] * caw[k:k + 1, :]
    y_a = a_b * conv
    tail_a = upad[8 + tl - pa_rows:8 + tl, :]
    upad[8 - pa_rows:8, :] = tail_a

    ps = jnp.dot(xn, win_ref[:, COL_A:D_IN_PAD], preferred_element_type=F32)
    z = ps[:, 0:D_SSM]
    xpad[8:8 + tl, :] = ps[:, D_SSM:D_SSM + D_XBC]
    dt = _softplus(ps[:, D_SSM + D_XBC:COL_S] + dtb_ref[...])
    csw = csw_ref[...]
    acc = xpad[8 - ps_rows:8 - ps_rows + tl, :] * csw[0:1, :]
    for k in range(1, SSM_CONV_WIDTH):
        acc = acc + xpad[8 - ps_rows + k:8 - ps_rows + k + tl, :] * csw[k:k + 1, :]
    xc = _silu(acc + csb_ref[...])
    tail_s = xpad[8 + tl - ps_rows:8 + tl, :]
    xpad[8 - ps_rows:8, :] = tail_s
    a_neg = -jnp.exp(alog_ref[...])
    tril = tril_ref[...]
    e2 = e2_ref[...]
    dskip = dsk_ref[...]
    nb = D_SSM + SSM_GROUPS * SSM_STATE
    for c in range(tl // q):
        r0 = c * q
        ys_scr[r0:r0 + q, :] = _ssd_chunk(
            xc[r0:r0 + q, 0:D_SSM], xc[r0:r0 + q, D_SSM:nb], xc[r0:r0 + q, nb:D_XBC],
            dt[r0:r0 + q, :], a_neg, z[r0:r0 + q, :], dskip, s_scr, tril, e2)

    mixed = jnp.concatenate([y_a, ys_scr[...]], axis=1)
    mean = jnp.dot((mixed * mixed).astype(BF16), g_ref[...], preferred_element_type=F32)
    r_hi, r_mid = _split2(lax.rsqrt(mean + EPS))
    rs_e = jnp.dot(jnp.concatenate([r_hi, r_mid], axis=1), e3_ref[...], preferred_element_type=F32)
    gam = jnp.concatenate([na_ref[...], ns_ref[...]], axis=1)
    mixed_n = (mixed * rs_e * gam).astype(BF16)
    x1 = x + jnp.dot(mixed_n, wout_ref[...], preferred_element_type=F32)
    x1_ref[...] = x1

    xf = x1 * lax.rsqrt(jnp.mean(x1 * x1, axis=-1, keepdims=True) + EPS) * nffn_ref[...]
    xf_ref[...] = xf
    f_hi, f_mid = _split2(xf)
    logits = jnp.dot(jnp.concatenate([f_hi, f_hi, f_mid], axis=1), wr_ref[...],
                     preferred_element_type=F32) + br_ref[...]
    lane = lax.broadcasted_iota(jnp.int32, (tl, LANES), 1)
    work = logits
    vals, idxs = [], []
    for _ in range(TOP_K):
        m = jnp.max(work, axis=-1, keepdims=True)
        idx = jnp.min(jnp.where(work == m, lane, LANES), axis=-1, keepdims=True)
        vals.append(m)
        idxs.append(idx)
        work = jnp.where(lane == idx, -jnp.inf, work)
    es = [jnp.exp(v - vals[0]) for v in vals]
    inv = 1.0 / (es[0] + es[1] + es[2] + es[3])
    route = jnp.zeros((tl, LANES), F32)
    for k in range(TOP_K):
        route = jnp.where(lane == k, idxs[k].astype(F32), route)
        route = jnp.where(lane == TOP_K + k, es[k] * inv, route)
    rrow_ref[...] = route
    route_ref[...] = _transpose_rows(route)[0:2 * TOP_K, :]

    @pl.when(l == last)
    def _():
        ca_ref[...] = tail_a
        cs_ref[...] = tail_s
        ssm_ref[...] = s_scr[...]


def _const_spec(shape):
    nd = len(shape)
    return pl.BlockSpec(shape, lambda b, l: (0,) * nd, pipeline_mode=pl.Buffered(1))


def _mixer_call(x, ca0, cs0, ssm0, wts, consts, *, tl, q, t_all, row_off, alias=None):
    b_sz, l_sz, _ = x.shape
    nl = l_sz // tl
    blk_off = row_off // tl
    n_alias = 0 if alias is None else len(alias)
    in_specs = [
        pl.BlockSpec((None, tl, D_MODEL), lambda b, l: (b, l, 0)),
        pl.BlockSpec((None, CONV_A_WIDTH - 1, D_CONV), lambda b, l: (b, 0, 0)),
        pl.BlockSpec((None, SSM_CONV_WIDTH - 1, D_XBC), lambda b, l: (b, 0, 0)),
        pl.BlockSpec((None, SSM_GROUPS, SSM_STATE, D_SSM // SSM_GROUPS), lambda b, l: (b, 0, 0, 0)),
    ] + [_const_spec(w.shape) for w in wts] + [_const_spec(c.shape) for c in consts]
    in_specs += [pl.BlockSpec(memory_space=pl.ANY)] * n_alias
    row_spec = pl.BlockSpec((tl, D_MODEL), lambda b, l: (blk_off + b * nl + l, 0))
    out_specs = [
        row_spec, row_spec,
        pl.BlockSpec((tl, LANES), lambda b, l: (blk_off + b * nl + l, 0)),
        pl.BlockSpec((None, 2 * TOP_K, tl), lambda b, l: (b, 0, l)),
        pl.BlockSpec((None, CONV_A_WIDTH - 1, D_CONV), lambda b, l: (b, 0, 0)),
        pl.BlockSpec((None, SSM_CONV_WIDTH - 1, D_XBC), lambda b, l: (b, 0, 0)),
        pl.BlockSpec((None, SSM_GROUPS, SSM_STATE, D_SSM // SSM_GROUPS), lambda b, l: (b, 0, 0, 0)),
    ]
    out_shape = [
        jax.ShapeDtypeStruct((t_all, D_MODEL), F32),
        jax.ShapeDtypeStruct((t_all, D_MODEL), F32),
        jax.ShapeDtypeStruct((t_all, LANES), F32),
        jax.ShapeDtypeStruct((b_sz, 2 * TOP_K, l_sz), F32),
        jax.ShapeDtypeStruct((b_sz, CONV_A_WIDTH - 1, D_CONV), F32),
        jax.ShapeDtypeStruct((b_sz, SSM_CONV_WIDTH - 1, D_XBC), F32),
        jax.ShapeDtypeStruct((b_sz, SSM_GROUPS, SSM_STATE, D_SSM // SSM_GROUPS), F32),
    ]
    n_in = 4 + len(wts) + len(consts)
    aliases = {n_in + i: i for i in range(n_alias)}
    args = [x, ca0, cs0, ssm0, *wts, *consts] + ([] if alias is None else list(alias))
    return pl.pallas_call(
        functools.partial(_mixer_kernel, tl=tl, q=q, nl=nl, n_alias=n_alias),
        grid=(b_sz, nl),
        in_specs=in_specs,
        out_specs=out_specs,
        out_shape=out_shape,
        scratch_shapes=[
            pltpu.VMEM((tl + 8, D_CONV), F32),
            pltpu.VMEM((tl + 8, D_XBC), F32),
            pltpu.VMEM((SSM_GROUPS, SSM_STATE, D_SSM // SSM_GROUPS), F32),
            pltpu.VMEM((tl, D_SSM), F32),
        ],
        input_output_aliases=aliases,
        compiler_params=pltpu.CompilerParams(
            dimension_semantics=("arbitrary", "arbitrary"), vmem_limit_bytes=VMEM_LIMIT),
        name="mixer",
    )(*args)


def _mixer_consts(q):
    r = jnp.arange(q)
    tril = (r[:, None] >= r[None, :]).astype(BF16)
    j2 = jnp.arange(2 * LANES) % LANES
    e2 = (j2[:, None] == (jnp.arange(D_SSM) // SSM_HEAD_DIM)[None, :]).astype(BF16)
    c = jnp.arange(D_CONV + D_SSM)
    grp = jnp.where(c < D_CONV, c // (D_CONV // CONV_GROUPS),
                    CONV_GROUPS + (c - D_CONV) // (D_SSM // SSM_GROUPS))
    wgt = jnp.where(c < D_CONV, CONV_GROUPS / D_CONV, SSM_GROUPS / D_SSM)
    g = jnp.where(grp[:, None] == jnp.arange(LANES)[None, :], wgt[:, None], 0.0).astype(BF16)
    e3 = (j2[:, None] == grp[None, :]).astype(BF16)
    return tril, e2, g, e3


def _prep_mixer_weights(norm_mix, w_in, conv_a_w, norm_conv_a, conv_ssm_w, conv_ssm_b, dt_bias, a_log,
                        d_skip, norm_ssm, w_out, norm_ffn, w_router, b_router):
    hpad = LANES - SSM_HEADS
    w_in_p = jnp.pad(w_in[0], ((0, 0), (0, hpad))).astype(BF16)
    wr = jnp.pad(w_router[0], ((0, 0), (0, LANES - N_EXPERTS)))
    wr_hi = wr.astype(BF16)
    wr_mid = (wr - wr_hi.astype(F32)).astype(BF16)
    wr3 = jnp.concatenate([wr_hi, wr_mid, wr_hi], axis=0)
    br = jnp.pad(b_router[0], (0, LANES - N_EXPERTS), constant_values=NEG_BIG)[None]
    return (norm_mix[0][None], w_in_p, conv_a_w[0], norm_conv_a[0][None], conv_ssm_w[0],
            conv_ssm_b[0][None], jnp.pad(dt_bias[0], (0, hpad))[None], jnp.pad(a_log[0], (0, hpad))[None],
            jnp.repeat(d_skip[0], SSM_HEAD_DIM)[None], norm_ssm[0][None], w_out[0].astype(BF16),
            norm_ffn[0][None], wr3, br)


def _ssm_to_internal(s):
    b = s.shape[0]
    s = s.reshape(b, SSM_GROUPS, SSM_HEADS // SSM_GROUPS, SSM_HEAD_DIM, SSM_STATE)
    return s.transpose(0, 1, 4, 2, 3).reshape(b, SSM_GROUPS, SSM_STATE, D_SSM // SSM_GROUPS)


def _ssm_from_internal(s):
    b = s.shape[0]
    s = s.reshape(b, SSM_GROUPS, SSM_STATE, SSM_HEADS // SSM_GROUPS, SSM_HEAD_DIM)
    return s.transpose(0, 1, 3, 4, 2).reshape(b, SSM_HEADS, SSM_HEAD_DIM, SSM_STATE)


MIX_TL = 256
MIX_Q = 128
COMB_TT = 256


def _routing(top_i):
    t = top_i.shape[0]
    n = t * TOP_K
    n_blocks = -(-n // MOE_BLOCK) + N_EXPERTS
    flat_e = top_i.reshape(n)
    onehot = (flat_e[:, None] == jnp.arange(N_EXPERTS, dtype=jnp.int32)[None, :]).astype(jnp.int32)
    csum = jnp.cumsum(onehot, axis=0)
    counts = csum[-1]
    rank = jnp.take_along_axis(csum, flat_e[:, None], axis=1)[:, 0] - 1
    padded = (counts + MOE_BLOCK - 1) // MOE_BLOCK * MOE_BLOCK
    padded_end = jnp.cumsum(padded)
    padded_start = padded_end - padded
    dest = padded_start[flat_e] + rank
    tok = jnp.arange(n, dtype=jnp.int32) // TOP_K
    slot_tok = jnp.zeros((n_blocks * MOE_BLOCK,), jnp.int32).at[dest].set(tok)
    blk_start = jnp.arange(n_blocks, dtype=jnp.int32) * MOE_BLOCK
    blk_expert = jnp.minimum(jnp.sum((padded_end[None, :] <= blk_start[:, None]).astype(jnp.int32), axis=1),
                             N_EXPERTS - 1)
    blk_valid = (blk_start < padded_end[-1]).astype(jnp.int32)
    return (slot_tok.reshape(n_blocks, 1, MOE_BLOCK), dest.astype(jnp.int32),
            blk_expert.astype(jnp.int32), blk_valid)


def _row_gather_start(src_hbm, idx_ref, n_rows, dst_of_row, sem):
    def body(i, carry):
        pltpu.make_async_copy(src_hbm.at[pl.ds(idx_ref[0, i], 1), :], dst_of_row(i), sem).start()
        return carry
    lax.fori_loop(0, n_rows, body, 0, unroll=8)


def _expert_kernel(be_ref, bv_ref, tok_cur, tok_nxt, xf_hbm, wg_ref, bg_ref, wu_ref, bu_ref, wd_ref, bd_ref,
                   yb_ref, xbuf, sem, wgb, wub, wdb, *, nb):
    b = pl.program_id(0)
    slot = b % 2

    def gather(idx_ref, s):
        _row_gather_start(xf_hbm, idx_ref, MOE_BLOCK, lambda i: xbuf.at[s, pl.ds(i, 1), :], sem.at[s])

    @pl.when(b == 0)
    def _():
        gather(tok_cur, 0)

    @pl.when(jnp.logical_and(b + 1 < nb, bv_ref[jnp.minimum(b + 1, nb - 1)] > 0))
    def _():
        gather(tok_nxt, 1 - slot)

    @pl.when(bv_ref[b] > 0)
    def _():
        e_prev = be_ref[jnp.maximum(b - 1, 0)]

        @pl.when(jnp.logical_or(b == 0, be_ref[b] != e_prev))
        def _():
            wgb[...] = wg_ref[...].astype(BF16)
            wub[...] = wu_ref[...].astype(BF16)
            wdb[...] = wd_ref[...].astype(BF16)

        pltpu.make_async_copy(xf_hbm.at[pl.ds(0, MOE_BLOCK), :], xbuf.at[slot], sem.at[slot]).wait()
        xb = xbuf[slot].astype(BF16)
        gate = jnp.dot(xb, wgb[...], preferred_element_type=F32) + bg_ref[...]
        up = jnp.dot(xb, wub[...], preferred_element_type=F32) + bu_ref[...]
        gate = jnp.minimum(gate, SWIGLU_LIMIT)
        up = jnp.clip(up, -SWIGLU_LIMIT, SWIGLU_LIMIT)
        glu = gate * jax.nn.sigmoid(gate * SWIGLU_ALPHA)
        h = ((up + 1.0) * glu).astype(BF16)
        yb_ref[...] = jnp.dot(h, wdb[...], preferred_element_type=F32) + bd_ref[...]

    @pl.when(bv_ref[b] == 0)
    def _():
        yb_ref[...] = jnp.zeros_like(yb_ref)


def _expert_call(slot_tok, blk_expert, blk_valid, xf, w_gate, b_gate, w_up, b_up, w_down, b_down):
    n_blocks = slot_tok.shape[0]
    d, de = w_gate.shape[1], w_gate.shape[2]
    tok_spec = lambda f: pl.BlockSpec((None, 1, MOE_BLOCK), f, memory_space=pltpu.SMEM)
    w_spec = lambda r, c: pl.BlockSpec((None, r, c), lambda b, be, bv: (be[b], 0, 0))
    grid_spec = pltpu.PrefetchScalarGridSpec(
        num_scalar_prefetch=2,
        grid=(n_blocks,),
        in_specs=[
            tok_spec(lambda b, be, bv: (b, 0, 0)),
            tok_spec(lambda b, be, bv: (jnp.minimum(b + 1, n_blocks - 1), 0, 0)),
            pl.BlockSpec(memory_space=pl.ANY),
            w_spec(d, de), w_spec(1, de), w_spec(d, de), w_spec(1, de), w_spec(de, d), w_spec(1, d),
        ],
        out_specs=pl.BlockSpec((MOE_BLOCK, d), lambda b, be, bv: (b, 0)),
        scratch_shapes=[
            pltpu.VMEM((2, MOE_BLOCK, d), F32),
            pltpu.SemaphoreType.DMA((2,)),
            pltpu.VMEM((d, de), BF16), pltpu.VMEM((d, de), BF16), pltpu.VMEM((de, d), BF16),
        ],
    )
    return pl.pallas_call(
        functools.partial(_expert_kernel, nb=n_blocks),
        grid_spec=grid_spec,
        out_shape=jax.ShapeDtypeStruct((n_blocks * MOE_BLOCK, d), F32),
        compiler_params=pltpu.CompilerParams(dimension_semantics=("arbitrary",), vmem_limit_bytes=VMEM_LIMIT),
        name="experts",
    )(blk_expert, blk_valid, slot_tok, slot_tok, xf, w_gate, b_gate[:, None, :], w_up, b_up[:, None, :],
      w_down, b_down[:, None, :])


def _combine_kernel(pos_cur, pos_nxt, x1_ref, rrow_ref, yb_hbm, fn_ref, out_ref, ybuf, sem, *, n):
    i = pl.program_id(0)
    slot = i % 2
    tt = x1_ref.shape[0]

    def gather(idx_ref, s):
        _row_gather_start(yb_hbm, idx_ref, tt * TOP_K,
                          lambda j: ybuf.at[s, lax.rem(j, TOP_K), pl.ds(lax.div(j, TOP_K), 1), :], sem.at[s])

    @pl.when(i == 0)
    def _():
        gather(pos_cur, 0)

    @pl.when(i + 1 < n)
    def _():
        gather(pos_nxt, 1 - slot)

    for k in range(TOP_K):
        pltpu.make_async_copy(yb_hbm.at[pl.ds(0, tt), :], ybuf.at[slot, k], sem.at[slot]).wait()
    rr = rrow_ref[...]
    acc = x1_ref[...]
    for k in range(TOP_K):
        acc = acc + rr[:, TOP_K + k:TOP_K + k + 1] * ybuf[slot, k]
    out_ref[...] = acc * lax.rsqrt(jnp.mean(acc * acc, axis=-1, keepdims=True) + EPS) * fn_ref[...]


def _combine_call(pos, x1, rrow, yb, fn, *, row_off, n_rows):
    tt = min(COMB_TT, n_rows)
    n_tiles = n_rows // tt
    off = row_off // tt
    d = x1.shape[1]
    pos = pos[row_off * TOP_K:(row_off + n_rows) * TOP_K].reshape(n_tiles, 1, tt * TOP_K)
    pos_spec = lambda f: pl.BlockSpec((None, 1, tt * TOP_K), f, memory_space=pltpu.SMEM)
    return pl.pallas_call(
        functools.partial(_combine_kernel, n=n_tiles),
        grid=(n_tiles,),
        in_specs=[
            pos_spec(lambda i: (i, 0, 0)),
            pos_spec(lambda i: (jnp.minimum(i + 1, n_tiles - 1), 0, 0)),
            pl.BlockSpec((tt, d), lambda i: (off + i, 0)),
            pl.BlockSpec((tt, LANES), lambda i: (off + i, 0)),
            pl.BlockSpec(memory_space=pl.ANY),
            pl.BlockSpec((1, d), lambda i: (0, 0)),
        ],
        out_specs=pl.BlockSpec((tt, d), lambda i: (i, 0)),
        out_shape=jax.ShapeDtypeStruct((n_rows, d), F32),
        scratch_shapes=[pltpu.VMEM((2, TOP_K, tt, d), F32), pltpu.SemaphoreType.DMA((2,))],
        compiler_params=pltpu.CompilerParams(dimension_semantics=("arbitrary",), vmem_limit_bytes=VMEM_LIMIT),
        name="combine",
    )(pos, pos, x1, rrow, yb, fn)


def kernel(x_prompt, x_sample, cache_conv_a, cache_conv_ssm, state_ssm, norm_mix, w_in, conv_a_w, norm_conv_a, conv_ssm_w, conv_ssm_b, dt_bias, a_log, d_skip, norm_ssm, w_out, norm_ffn, w_router, b_router, w_gate, b_gate, w_up, b_up, w_down, b_down, final_norm):
    bp, lp, _ = x_prompt.shape
    bs, ls, _ = x_sample.shape
    t_p, t_s = bp * lp, bs * ls
    t_all = t_p + t_s
    wts = _prep_mixer_weights(norm_mix, w_in, conv_a_w, norm_conv_a, conv_ssm_w, conv_ssm_b, dt_bias, a_log,
                              d_skip, norm_ssm, w_out, norm_ffn, w_router, b_router)
    zeros = lambda *s: jnp.zeros(s, F32)
    x1, xf, rrow, route_p, ca_p, cs_p, st_p = _mixer_call(
        x_prompt, zeros(bp, CONV_A_WIDTH - 1, D_CONV), zeros(bp, SSM_CONV_WIDTH - 1, D_XBC),
        zeros(bp, SSM_GROUPS, SSM_STATE, D_SSM // SSM_GROUPS), wts, _mixer_consts(MIX_Q),
        tl=MIX_TL, q=MIX_Q, t_all=t_all, row_off=0)
    x1, xf, rrow, route_s, ca_s, cs_s, st_s = _mixer_call(
        x_sample, cache_conv_a[0], cache_conv_ssm[0], _ssm_to_internal(state_ssm[0]), wts, _mixer_consts(ls),
        tl=ls, q=ls, t_all=t_all, row_off=t_p, alias=(x1, xf, rrow))

    top_i = jnp.concatenate([route_p[:, 0:TOP_K, :].transpose(0, 2, 1).reshape(t_p, TOP_K),
                             route_s[:, 0:TOP_K, :].transpose(0, 2, 1).reshape(t_s, TOP_K)], axis=0)
    slot_tok, pos, blk_expert, blk_valid = _routing(top_i.astype(jnp.int32))

    yb = _expert_call(slot_tok, blk_expert, blk_valid, xf, w_gate[0], b_gate[0], w_up[0], b_up[0],
                      w_down[0], b_down[0])
    fn = final_norm[None]
    y_p = _combine_call(pos, x1, rrow, yb, fn, row_off=0, n_rows=t_p)
    y_s = _combine_call(pos, x1, rrow, yb, fn, row_off=t_p, n_rows=t_s)
    return (y_p.reshape(bp, lp, D_MODEL), y_s.reshape(bs, ls, D_MODEL),
            ca_p[None], cs_p[None], _ssm_from_internal(st_p)[None],
            ca_s[None], cs_s[None], _ssm_from_internal(st_s)[None])
```

```python
import functools

import jax
import jax.numpy as jnp
from jax import lax
from jax.experimental import pallas as pl
from jax.experimental.pallas import tpu as pltpu

F32 = jnp.float32
BF16 = jnp.bfloat16

D_MODEL = 1024
D_CONV = 1024
D_SSM = 1024
CONV_GROUPS = 16
CONV_A_WIDTH = 3
SSM_HEAD_DIM = 64
SSM_HEADS = 16
SSM_GROUPS = 2
SSM_STATE = 128
SSM_CONV_WIDTH = 4
D_XBC = D_SSM + 2 * SSM_GROUPS * SSM_STATE
N_EXPERTS = 32
TOP_K = 4
SWIGLU_LIMIT = 7.0
SWIGLU_ALPHA = 1.702
MOE_BLOCK = 256
EPS = 1e-5

LANES = 128
COL_A = 3 * D_CONV
COL_S = D_SSM + D_XBC + LANES
D_IN_PAD = COL_A + COL_S
NEG_BIG = -1e30
VMEM_LIMIT = 56 * 1024 * 1024


def _softplus(v):
    return jnp.maximum(v, 0.0) + jnp.log1p(jnp.exp(-jnp.abs(v)))


def _silu(v):
    return v * jax.nn.sigmoid(v)


def _split2(v):
    hi = v.astype(BF16)
    mid = (v - hi.astype(F32)).astype(BF16)
    return hi, mid


ROW_TILES = D_MODEL // LANES


def _store_token_tiles(ref, v):
    r = v.shape[0]
    for s in range(ROW_TILES):
        ref[pl.ds(s, r, stride=ROW_TILES), :] = v[:, s * LANES:(s + 1) * LANES]


def _load_token_tiles(ref, r):
    return jnp.concatenate([ref[pl.ds(s, r, stride=ROW_TILES), :] for s in range(ROW_TILES)], axis=1)


def _transpose_rows(v):
    r = v.shape[0]
    if r % LANES:
        v = jnp.concatenate([v, jnp.zeros((LANES - r % LANES, v.shape[1]), v.dtype)], axis=0)
    return v.T[:, :r]


def _ssd_chunk(xs, bm, cm, dt, a_neg, z, dskip, s_scr, tril, e2):
    q = xs.shape[0]
    da = dt * a_neg
    d_hi = da.astype(BF16)
    r1 = da - d_hi.astype(F32)
    d_mid = r1.astype(BF16)
    d_lo = (r1 - d_mid.astype(F32)).astype(BF16)
    cs = jnp.dot(tril, jnp.concatenate([d_hi, d_mid, d_lo], axis=1), preferred_element_type=F32)
    acum = cs[:, 0:LANES] + cs[:, LANES:2 * LANES] + cs[:, 2 * LANES:3 * LANES]
    acum_t = _transpose_rows(acum)
    total = acum[q - 1:q, :]
    stack = jnp.concatenate(
        [dt, jnp.exp(total - acum), jnp.exp(acum), jnp.broadcast_to(jnp.exp(total), (8, LANES))], axis=0)
    s_hi, s_mid = _split2(stack)
    ex = jnp.dot(jnp.concatenate([s_hi, s_mid], axis=1), e2, preferred_element_type=F32)
    dt_e, dec_e, ea_e, cd_e = ex[0:q], ex[q:2 * q], ex[2 * q:3 * q], ex[3 * q:3 * q + 1]
    xdt = xs * dt_e
    xdt_b = xdt.astype(BF16)
    xdec_b = (xdt * dec_e).astype(BF16)
    row = lax.broadcasted_iota(jnp.int32, (q, q), 0)
    col = lax.broadcasted_iota(jnp.int32, (q, q), 1)
    causal = row >= col
    lane = lax.broadcasted_iota(jnp.int32, (q, LANES), 1)
    gw = D_SSM // SSM_GROUPS
    hpg = SSM_HEADS // SSM_GROUPS
    ys = []
    for g in range(SSM_GROUPS):
        cg = cm[:, g * SSM_STATE:(g + 1) * SSM_STATE].astype(BF16)
        bg = bm[:, g * SSM_STATE:(g + 1) * SSM_STATE].astype(BF16)
        cb = lax.dot_general(cg, bg, (((1,), (1,)), ((), ())), preferred_element_type=F32)
        s_g = s_scr[g]
        y_off = jnp.dot(cg, s_g.astype(BF16), preferred_element_type=F32) * ea_e[:, g * gw:(g + 1) * gw]
        pieces = []
        for j in range(hpg // 2):
            h0 = g * hpg + 2 * j
            ms = []
            for h in (h0, h0 + 1):
                diff = acum[:, h:h + 1] - acum_t[h:h + 1, :]
                ms.append(cb * jnp.exp(jnp.where(causal, diff, NEG_BIG)))
            m = jnp.concatenate(ms, axis=1).astype(BF16)
            xp = xdt_b[:, h0 * SSM_HEAD_DIM:(h0 + 2) * SSM_HEAD_DIM]
            zero = jnp.zeros_like(xp)
            rhs = jnp.concatenate([jnp.where(lane < SSM_HEAD_DIM, xp, zero),
                                   jnp.where(lane >= SSM_HEAD_DIM, xp, zero)], axis=0)
            pieces.append(jnp.dot(m, rhs, preferred_element_type=F32))
        ys.append(jnp.concatenate(pieces, axis=1) + y_off)
        upd = lax.dot_general(bg, xdec_b[:, g * gw:(g + 1) * gw], (((0,), (0,)), ((), ())),
                              preferred_element_type=F32)
        s_scr[g] = s_g * cd_e[:, g * gw:(g + 1) * gw] + upd
    y = jnp.concatenate(ys, axis=1) + dskip * xs
    return y * _silu(z)


def _mixer_kernel(*refs, tl, q, nl, n_alias):
    (x_ref, ca0_ref, cs0_ref, ssm0_ref, nmix_ref, win_ref, caw_ref, na_ref, csw_ref, csb_ref,
     dtb_ref, alog_ref, dsk_ref, ns_ref, wout_ref, nffn_ref, wr_ref, br_ref,
     tril_ref, e2_ref, g_ref, e3_ref) = refs[:22]
    refs = refs[22 + n_alias:]
    x1_ref, xf_ref, rrow_ref, route_ref, ca_ref, cs_ref, ssm_ref, upad, xpad, s_scr, ys_scr = refs
    l = pl.program_id(1)
    last = nl - 1
    pa_rows = CONV_A_WIDTH - 1
    ps_rows = SSM_CONV_WIDTH - 1

    @pl.when(l == 0)
    def _():
        upad[8 - pa_rows:8, :] = ca0_ref[...]
        xpad[8 - ps_rows:8, :] = cs0_ref[...]
        s_scr[...] = ssm0_ref[...]

    x = x_ref[...]
    xn = (x * lax.rsqrt(jnp.mean(x * x, axis=-1, keepdims=True) + EPS) * nmix_ref[...]).astype(BF16)

    pa = jnp.dot(xn, win_ref[:, 0:COL_A], preferred_element_type=F32)
    a_b, a_c, a_h = pa[:, 0:D_CONV], pa[:, D_CONV:2 * D_CONV], pa[:, 2 * D_CONV:3 * D_CONV]
    upad[8:8 + tl, :] = a_c * a_h
    caw = caw_ref[...]
    conv = upad[8 - pa_rows:8 - pa_rows + tl, :] * caw[0:1, :]
    for k in range(1, CONV_A_WIDTH):
        conv = conv + upad[8 - pa_rows + k:8 - pa_rows + k + tl, :] * caw[k:k + 1, :]
    y_a = a_b * conv
    tail_a = upad[8 + tl - pa_rows:8 + tl, :]
    upad[8 - pa_rows:8, :] = tail_a

    ps = jnp.dot(xn, win_ref[:, COL_A:D_IN_PAD], preferred_element_type=F32)
    z = ps[:, 0:D_SSM]
    xpad[8:8 + tl, :] = ps[:, D_SSM:D_SSM + D_XBC]
    dt = _softplus(ps[:, D_SSM + D_XBC:COL_S] + dtb_ref[...])
    csw = csw_ref[...]
    acc = xpad[8 - ps_rows:8 - ps_rows + tl, :] * csw[0:1, :]
    for k in range(1, SSM_CONV_WIDTH):
        acc = acc + xpad[8 - ps_rows + k:8 - ps_rows + k + tl, :] * csw[k:k + 1, :]
    xc = _silu(acc + csb_ref[...])
    tail_s = xpad[8 + tl - ps_rows:8 + tl, :]
    xpad[8 - ps_rows:8, :] = tail_s
    a_neg = -jnp.exp(alog_ref[...])
    tril = tril_ref[...]
    e2 = e2_ref[...]
    dskip = dsk_ref[...]
    nb = D_SSM + SSM_GROUPS * SSM_STATE
    for c in range(tl // q):
        r0 = c * q
        ys_scr[r0:r0 + q, :] = _ssd_chunk(
            xc[r0:r0 + q, 0:D_SSM], xc[r0:r0 + q, D_SSM:nb], xc[r0:r0 + q, nb:D_XBC],
            dt[r0:r0 + q, :], a_neg, z[r0:r0 + q, :], dskip, s_scr, tril, e2)

    mixed = jnp.concatenate([y_a, ys_scr[...]], axis=1)
    mean = jnp.dot((mixed * mixed).astype(BF16), g_ref[...], preferred_element_type=F32)
    r_hi, r_mid = _split2(lax.rsqrt(mean + EPS))
    rs_e = jnp.dot(jnp.concatenate([r_hi, r_mid], axis=1), e3_ref[...], preferred_element_type=F32)
    gam = jnp.concatenate([na_ref[...], ns_ref[...]], axis=1)
    mixed_n = (mixed * rs_e * gam).astype(BF16)
    x1 = x + jnp.dot(mixed_n, wout_ref[...], preferred_element_type=F32)
    x1_ref[...] = x1

    xf = x1 * lax.rsqrt(jnp.mean(x1 * x1, axis=-1, keepdims=True) + EPS) * nffn_ref[...]
    _store_token_tiles(xf_ref, xf)
    f_hi, f_mid = _split2(xf)
    logits = jnp.dot(jnp.concatenate([f_hi, f_hi, f_mid], axis=1), wr_ref[...],
                     preferred_element_type=F32) + br_ref[...]
    lane = lax.broadcasted_iota(jnp.int32, (tl, LANES), 1)
    work = logits
    vals, idxs = [], []
    for _ in range(TOP_K):
        m = jnp.max(work, axis=-1, keepdims=True)
        idx = jnp.min(jnp.where(work == m, lane, LANES), axis=-1, keepdims=True)
        vals.append(m)
        idxs.append(idx)
        work = jnp.where(lane == idx, -jnp.inf, work)
    es = [jnp.exp(v - vals[0]) for v in vals]
    inv = 1.0 / (es[0] + es[1] + es[2] + es[3])
    route = jnp.zeros((tl, LANES), F32)
    for k in range(TOP_K):
        route = jnp.where(lane == k, idxs[k].astype(F32), route)
        route = jnp.where(lane == TOP_K + k, es[k] * inv, route)
    rrow_ref[...] = route
    route_ref[...] = _transpose_rows(route)[0:2 * TOP_K, :]

    @pl.when(l == last)
    def _():
        ca_ref[...] = tail_a
        cs_ref[...] = tail_s
        ssm_ref[...] = s_scr[...]


def _const_spec(shape):
    nd = len(shape)
    return pl.BlockSpec(shape, lambda b, l: (0,) * nd, pipeline_mode=pl.Buffered(1))


def _mixer_call(x, ca0, cs0, ssm0, wts, consts, *, tl, q, t_all, row_off, alias=None):
    b_sz, l_sz, _ = x.shape
    nl = l_sz // tl
    blk_off = row_off // tl
    n_alias = 0 if alias is None else len(alias)
    in_specs = [
        pl.BlockSpec((None, tl, D_MODEL), lambda b, l: (b, l, 0)),
        pl.BlockSpec((None, CONV_A_WIDTH - 1, D_CONV), lambda b, l: (b, 0, 0)),
        pl.BlockSpec((None, SSM_CONV_WIDTH - 1, D_XBC), lambda b, l: (b, 0, 0)),
        pl.BlockSpec((None, SSM_GROUPS, SSM_STATE, D_SSM // SSM_GROUPS), lambda b, l: (b, 0, 0, 0)),
    ] + [_const_spec(w.shape) for w in wts] + [_const_spec(c.shape) for c in consts]
    in_specs += [pl.BlockSpec(memory_space=pl.ANY)] * n_alias
    row_spec = pl.BlockSpec((tl, D_MODEL), lambda b, l: (blk_off + b * nl + l, 0))
    out_specs = [
        row_spec,
        pl.BlockSpec((tl * ROW_TILES, LANES), lambda b, l: (blk_off + b * nl + l, 0)),
        pl.BlockSpec((tl, LANES), lambda b, l: (blk_off + b * nl + l, 0)),
        pl.BlockSpec((None, 2 * TOP_K, tl), lambda b, l: (b, 0, l)),
        pl.BlockSpec((None, CONV_A_WIDTH - 1, D_CONV), lambda b, l: (b, 0, 0)),
        pl.BlockSpec((None, SSM_CONV_WIDTH - 1, D_XBC), lambda b, l: (b, 0, 0)),
        pl.BlockSpec((None, SSM_GROUPS, SSM_STATE, D_SSM // SSM_GROUPS), lambda b, l: (b, 0, 0, 0)),
    ]
    out_shape = [
        jax.ShapeDtypeStruct((t_all, D_MODEL), F32),
        jax.ShapeDtypeStruct((t_all * ROW_TILES, LANES), F32),
        jax.ShapeDtypeStruct((t_all, LANES), F32),
        jax.ShapeDtypeStruct((b_sz, 2 * TOP_K, l_sz), F32),
        jax.ShapeDtypeStruct((b_sz, CONV_A_WIDTH - 1, D_CONV), F32),
        jax.ShapeDtypeStruct((b_sz, SSM_CONV_WIDTH - 1, D_XBC), F32),
        jax.ShapeDtypeStruct((b_sz, SSM_GROUPS, SSM_STATE, D_SSM // SSM_GROUPS), F32),
    ]
    n_in = 4 + len(wts) + len(consts)
    aliases = {n_in + i: i for i in range(n_alias)}
    args = [x, ca0, cs0, ssm0, *wts, *consts] + ([] if alias is None else list(alias))
    return pl.pallas_call(
        functools.partial(_mixer_kernel, tl=tl, q=q, nl=nl, n_alias=n_alias),
        grid=(b_sz, nl),
        in_specs=in_specs,
        out_specs=out_specs,
        out_shape=out_shape,
        scratch_shapes=[
            pltpu.VMEM((tl + 8, D_CONV), F32),
            pltpu.VMEM((tl + 8, D_XBC), F32),
            pltpu.VMEM((SSM_GROUPS, SSM_STATE, D_SSM // SSM_GROUPS), F32),
            pltpu.VMEM((tl, D_SSM), F32),
        ],
        input_output_aliases=aliases,
        compiler_params=pltpu.CompilerParams(
            dimension_semantics=("arbitrary", "arbitrary"), vmem_limit_bytes=VMEM_LIMIT),
        name="mixer",
    )(*args)


def _mixer_consts(q):
    r = jnp.arange(q)
    tril = (r[:, None] >= r[None, :]).astype(BF16)
    j2 = jnp.arange(2 * LANES) % LANES
    e2 = (j2[:, None] == (jnp.arange(D_SSM) // SSM_HEAD_DIM)[None, :]).astype(BF16)
    c = jnp.arange(D_CONV + D_SSM)
    grp = jnp.where(c < D_CONV, c // (D_CONV // CONV_GROUPS),
                    CONV_GROUPS + (c - D_CONV) // (D_SSM // SSM_GROUPS))
    wgt = jnp.where(c < D_CONV, CONV_GROUPS / D_CONV, SSM_GROUPS / D_SSM)
    g = jnp.where(grp[:, None] == jnp.arange(LANES)[None, :], wgt[:, None], 0.0).astype(BF16)
    e3 = (j2[:, None] == grp[None, :]).astype(BF16)
    return tril, e2, g, e3


def _prep_mixer_weights(norm_mix, w_in, conv_a_w, norm_conv_a, conv_ssm_w, conv_ssm_b, dt_bias, a_log,
                        d_skip, norm_ssm, w_out, norm_ffn, w_router, b_router):
    hpad = LANES - SSM_HEADS
    w_in_p = jnp.pad(w_in[0], ((0, 0), (0, hpad))).astype(BF16)
    wr = jnp.pad(w_router[0], ((0, 0), (0, LANES - N_EXPERTS)))
    wr_hi = wr.astype(BF16)
    wr_mid = (wr - wr_hi.astype(F32)).astype(BF16)
    wr3 = jnp.concatenate([wr_hi, wr_mid, wr_hi], axis=0)
    br = jnp.pad(b_router[0], (0, LANES - N_EXPERTS), constant_values=NEG_BIG)[None]
    return (norm_mix[0][None], w_in_p, conv_a_w[0], norm_conv_a[0][None], conv_ssm_w[0],
            conv_ssm_b[0][None], jnp.pad(dt_bias[0], (0, hpad))[None], jnp.pad(a_log[0], (0, hpad))[None],
            jnp.repeat(d_skip[0], SSM_HEAD_DIM)[None], norm_ssm[0][None], w_out[0].astype(BF16),
            norm_ffn[0][None], wr3, br)


def _ssm_to_internal(s):
    b = s.shape[0]
    s = s.reshape(b, SSM_GROUPS, SSM_HEADS // SSM_GROUPS, SSM_HEAD_DIM, SSM_STATE)
    return s.transpose(0, 1, 4, 2, 3).reshape(b, SSM_GROUPS, SSM_STATE, D_SSM // SSM_GROUPS)


def _ssm_from_internal(s):
    b = s.shape[0]
    s = s.reshape(b, SSM_GROUPS, SSM_STATE, SSM_HEADS // SSM_GROUPS, SSM_HEAD_DIM)
    return s.transpose(0, 1, 3, 4, 2).reshape(b, SSM_HEADS, SSM_HEAD_DIM, SSM_STATE)


MIX_TL = 256
MIX_Q = 128
COMB_TT = 256


def _routing(top_i):
    t = top_i.shape[0]
    n = t * TOP_K
    assert n % MOE_BLOCK == 0
    nblk = n // MOE_BLOCK
    n_items = nblk + N_EXPERTS
    shift = max(n - 1, 1).bit_length()
    assert N_EXPERTS << shift < 2 ** 31
    flat_e = top_i.reshape(n)
    ar = jnp.arange(n, dtype=jnp.int32)
    key = lax.sort(flat_e * (1 << shift) + ar)
    sorted_e = key >> shift
    slot_tok = (key & ((1 << shift) - 1)) // TOP_K
    experts = jnp.arange(N_EXPERTS, dtype=jnp.int32)
    group_start = jnp.sum((sorted_e[:, None] < experts[None, :]).astype(jnp.int32), axis=0)
    onehot = (flat_e[:, None] == experts[None, :]).astype(jnp.int32)
    rank = jnp.take_along_axis(jnp.cumsum(onehot, axis=0), flat_e[:, None], axis=1)[:, 0] - 1
    pos = group_start[flat_e] + rank
    bp = lax.sort(jnp.concatenate([jnp.arange(nblk, dtype=jnp.int32) * MOE_BLOCK, group_start]))
    dup = jnp.concatenate([jnp.zeros((1,), bool), bp[1:] == bp[:-1]])
    bp = lax.sort(jnp.where(jnp.logical_or(dup, bp >= n), n, bp))
    valid = bp < n
    blk = jnp.minimum(bp // MOE_BLOCK, nblk - 1)
    lo = jnp.where(valid, bp - blk * MOE_BLOCK, 0)
    nxt = jnp.concatenate([bp[1:], jnp.full((1,), n, jnp.int32)])
    hi = jnp.where(valid, jnp.minimum(nxt, (blk + 1) * MOE_BLOCK) - blk * MOE_BLOCK, 0)
    exp = sorted_e[jnp.minimum(bp, n - 1)]
    items = tuple(a.astype(jnp.int32) for a in (blk, exp, lo, hi, valid))
    assert items[0].shape == (n_items,)
    return slot_tok.astype(jnp.int32).reshape(nblk, 1, MOE_BLOCK), pos.astype(jnp.int32), items


def _tile_gather_start(src_hbm, idx_ref, idx_off, n_rows, dst, sem):
    def body(i, carry):
        src_row = pl.multiple_of(idx_ref[0, idx_off + i] * ROW_TILES, ROW_TILES)
        dst_row = pl.multiple_of(i * ROW_TILES, ROW_TILES)
        pltpu.make_async_copy(src_hbm.at[pl.ds(src_row, ROW_TILES), :],
                              dst.at[pl.ds(dst_row, ROW_TILES), :], sem).start()
        return carry
    lax.fori_loop(0, n_rows, body, 0, unroll=8)


def _expert_kernel(blk_ref, exp_ref, lo_ref, hi_ref, val_ref, tok_cur, tok_nxt, xf_hbm,
                   wg_ref, bg_ref, wu_ref, bu_ref, wd_ref, bd_ref, yb_ref, xbuf, sem, wgb, wub, wdb, *, nblk):
    w = pl.program_id(0)
    blk = blk_ref[w]
    lo = lo_ref[w]
    valid = val_ref[w] > 0
    first = lo == 0
    slot = blk % 2

    def gather(idx_ref, s):
        _tile_gather_start(xf_hbm, idx_ref, 0, MOE_BLOCK, xbuf.at[s], sem.at[s])

    @pl.when(w == 0)
    def _():
        gather(tok_cur, 0)

    @pl.when(jnp.logical_and(jnp.logical_and(valid, first), blk + 1 < nblk))
    def _():
        gather(tok_nxt, 1 - slot)

    @pl.when(valid)
    def _():
        @pl.when(jnp.logical_or(w == 0, exp_ref[w] != exp_ref[jnp.maximum(w - 1, 0)]))
        def _():
            wgb[...] = wg_ref[...].astype(BF16)
            wub[...] = wu_ref[...].astype(BF16)
            wdb[...] = wd_ref[...].astype(BF16)

        @pl.when(first)
        def _():
            pltpu.make_async_copy(xf_hbm.at[pl.ds(0, MOE_BLOCK * ROW_TILES), :], xbuf.at[slot],
                                  sem.at[slot]).wait()

        xb = _load_token_tiles(xbuf.at[slot], MOE_BLOCK).astype(BF16)
        gate = jnp.dot(xb, wgb[...], preferred_element_type=F32) + bg_ref[...]
        up = jnp.dot(xb, wub[...], preferred_element_type=F32) + bu_ref[...]
        gate = jnp.minimum(gate, SWIGLU_LIMIT)
        up = jnp.clip(up, -SWIGLU_LIMIT, SWIGLU_LIMIT)
        glu = gate * jax.nn.sigmoid(gate * SWIGLU_ALPHA)
        h = ((up + 1.0) * glu).astype(BF16)
        y = jnp.dot(h, wdb[...], preferred_element_type=F32) + bd_ref[...]

        @pl.when(first)
        def _():
            _store_token_tiles(yb_ref, y)

        @pl.when(jnp.logical_not(first))
        def _():
            keep = lax.broadcasted_iota(jnp.int32, (MOE_BLOCK, D_MODEL), 0) < lo
            _store_token_tiles(yb_ref, jnp.where(keep, _load_token_tiles(yb_ref, MOE_BLOCK), y))


def _expert_call(slot_tok, items, xf, w_gate, b_gate, w_up, b_up, w_down, b_down):
    nblk = slot_tok.shape[0]
    n_items = items[0].shape[0]
    d, de = w_gate.shape[1], w_gate.shape[2]
    tok_spec = lambda f: pl.BlockSpec((None, 1, MOE_BLOCK), f, memory_space=pltpu.SMEM)
    w_spec = lambda r, c: pl.BlockSpec((None, r, c), lambda w, blk, exp, lo, hi, val: (exp[w], 0, 0))
    grid_spec = pltpu.PrefetchScalarGridSpec(
        num_scalar_prefetch=5,
        grid=(n_items,),
        in_specs=[
            tok_spec(lambda w, blk, exp, lo, hi, val: (blk[w], 0, 0)),
            tok_spec(lambda w, blk, exp, lo, hi, val: (jnp.minimum(blk[w] + 1, nblk - 1), 0, 0)),
            pl.BlockSpec(memory_space=pl.ANY),
            w_spec(d, de), w_spec(1, de), w_spec(d, de), w_spec(1, de), w_spec(de, d), w_spec(1, d),
        ],
        out_specs=pl.BlockSpec((MOE_BLOCK * ROW_TILES, LANES), lambda w, blk, exp, lo, hi, val: (blk[w], 0)),
        scratch_shapes=[
            pltpu.VMEM((2, MOE_BLOCK * ROW_TILES, LANES), F32),
            pltpu.SemaphoreType.DMA((2,)),
            pltpu.VMEM((d, de), BF16), pltpu.VMEM((d, de), BF16), pltpu.VMEM((de, d), BF16),
        ],
    )
    return pl.pallas_call(
        functools.partial(_expert_kernel, nblk=nblk),
        grid_spec=grid_spec,
        out_shape=jax.ShapeDtypeStruct((nblk * MOE_BLOCK * ROW_TILES, LANES), F32),
        compiler_params=pltpu.CompilerParams(dimension_semantics=("arbitrary",), vmem_limit_bytes=VMEM_LIMIT),
        name="experts",
    )(*items, slot_tok, slot_tok, xf, w_gate, b_gate[:, None, :], w_up, b_up[:, None, :],
      w_down, b_down[:, None, :])


def _combine_kernel(pos_cur, pos_nxt, x1_ref, rrow_ref, yb_hbm, fn_ref, out_ref, ybuf, sem, *, n):
    i = pl.program_id(0)
    slot = i % 2
    tt = x1_ref.shape[0]

    def gather(idx_ref, s):
        for k in range(TOP_K):
            _tile_gather_start(yb_hbm, idx_ref, k * tt, tt, ybuf.at[s, k], sem.at[s])

    @pl.when(i == 0)
    def _():
        gather(pos_cur, 0)

    @pl.when(i + 1 < n)
    def _():
        gather(pos_nxt, 1 - slot)

    for k in range(TOP_K):
        pltpu.make_async_copy(yb_hbm.at[pl.ds(0, tt * ROW_TILES), :], ybuf.at[slot, k], sem.at[slot]).wait()
    rr = rrow_ref[...]
    acc = x1_ref[...]
    for k in range(TOP_K):
        acc = acc + rr[:, TOP_K + k:TOP_K + k + 1] * _load_token_tiles(ybuf.at[slot, k], tt)
    out_ref[...] = acc * lax.rsqrt(jnp.mean(acc * acc, axis=-1, keepdims=True) + EPS) * fn_ref[...]


def _combine_call(pos, x1, rrow, yb, fn, *, row_off, n_rows):
    tt = min(COMB_TT, n_rows)
    n_tiles = n_rows // tt
    off = row_off // tt
    d = x1.shape[1]
    pos = pos[row_off * TOP_K:(row_off + n_rows) * TOP_K].reshape(n_tiles, tt, TOP_K)
    pos = pos.transpose(0, 2, 1).reshape(n_tiles, 1, TOP_K * tt)
    pos_spec = lambda f: pl.BlockSpec((None, 1, tt * TOP_K), f, memory_space=pltpu.SMEM)
    return pl.pallas_call(
        functools.partial(_combine_kernel, n=n_tiles),
        grid=(n_tiles,),
        in_specs=[
            pos_spec(lambda i: (i, 0, 0)),
            pos_spec(lambda i: (jnp.minimum(i + 1, n_tiles - 1), 0, 0)),
            pl.BlockSpec((tt, d), lambda i: (off + i, 0)),
            pl.BlockSpec((tt, LANES), lambda i: (off + i, 0)),
            pl.BlockSpec(memory_space=pl.ANY),
            pl.BlockSpec((1, d), lambda i: (0, 0)),
        ],
        out_specs=pl.BlockSpec((tt, d), lambda i: (i, 0)),
        out_shape=jax.ShapeDtypeStruct((n_rows, d), F32),
        scratch_shapes=[pltpu.VMEM((2, TOP_K, tt * ROW_TILES, LANES), F32), pltpu.SemaphoreType.DMA((2,))],
        compiler_params=pltpu.CompilerParams(dimension_semantics=("arbitrary",), vmem_limit_bytes=VMEM_LIMIT),
        name="combine",
    )(pos, pos, x1, rrow, yb, fn)


def kernel(x_prompt, x_sample, cache_conv_a, cache_conv_ssm, state_ssm, norm_mix, w_in, conv_a_w, norm_conv_a, conv_ssm_w, conv_ssm_b, dt_bias, a_log, d_skip, norm_ssm, w_out, norm_ffn, w_router, b_router, w_gate, b_gate, w_up, b_up, w_down, b_down, final_norm):
    bp, lp, _ = x_prompt.shape
    bs, ls, _ = x_sample.shape
    t_p, t_s = bp * lp, bs * ls
    t_all = t_p + t_s
    wts = _prep_mixer_weights(norm_mix, w_in, conv_a_w, norm_conv_a, conv_ssm_w, conv_ssm_b, dt_bias, a_log,
                              d_skip, norm_ssm, w_out, norm_ffn, w_router, b_router)
    zeros = lambda *s: jnp.zeros(s, F32)
    x1, xf, rrow, route_p, ca_p, cs_p, st_p = _mixer_call(
        x_prompt, zeros(bp, CONV_A_WIDTH - 1, D_CONV), zeros(bp, SSM_CONV_WIDTH - 1, D_XBC),
        zeros(bp, SSM_GROUPS, SSM_STATE, D_SSM // SSM_GROUPS), wts, _mixer_consts(MIX_Q),
        tl=MIX_TL, q=MIX_Q, t_all=t_all, row_off=0)
    x1, xf, rrow, route_s, ca_s, cs_s, st_s = _mixer_call(
        x_sample, cache_conv_a[0], cache_conv_ssm[0], _ssm_to_internal(state_ssm[0]), wts, _mixer_consts(ls),
        tl=ls, q=ls, t_all=t_all, row_off=t_p, alias=(x1, xf, rrow))

    top_i = jnp.concatenate([route_p[:, 0:TOP_K, :].transpose(0, 2, 1).reshape(t_p, TOP_K),
                             route_s[:, 0:TOP_K, :].transpose(0, 2, 1).reshape(t_s, TOP_K)], axis=0)
    slot_tok, pos, items = _routing(top_i.astype(jnp.int32))

    yb = _expert_call(slot_tok, items, xf, w_gate[0], b_gate[0], w_up[0], b_up[0], w_down[0], b_down[0])
    fn = final_norm[None]
    y_p = _combine_call(pos, x1, rrow, yb, fn, row_off=0, n_rows=t_p)
    y_s = _combine_call(pos, x1, rrow, yb, fn, row_off=t_p, n_rows=t_s)
    return (y_p.reshape(bp, lp, D_MODEL), y_s.reshape(bs, ls, D_MODEL),
            ca_p[None], cs_p[None], _ssm_from_internal(st_p)[None],
            ca_s[None], cs_s[None], _ssm_from_internal(st_s)[None])
```

```python
import functools

import jax
import jax.numpy as jnp
from jax import lax
from jax.experimental import pallas as pl
from jax.experimental.pallas import tpu as pltpu

F32 = jnp.float32
BF16 = jnp.bfloat16

D_MODEL = 1024
D_CONV = 1024
D_SSM = 1024
CONV_GROUPS = 16
CONV_A_WIDTH = 3
SSM_HEAD_DIM = 64
SSM_HEADS = 16
SSM_GROUPS = 2
SSM_STATE = 128
SSM_CONV_WIDTH = 4
D_XBC = D_SSM + 2 * SSM_GROUPS * SSM_STATE
N_EXPERTS = 32
TOP_K = 4
SWIGLU_LIMIT = 7.0
SWIGLU_ALPHA = 1.702
MOE_BLOCK = 256
EPS = 1e-5

LANES = 128
COL_A = 3 * D_CONV
COL_S = D_SSM + D_XBC + LANES
D_IN_PAD = COL_A + COL_S
NEG_BIG = -1e30
VMEM_LIMIT = 56 * 1024 * 1024


def _softplus(v):
    return jnp.maximum(v, 0.0) + jnp.log1p(jnp.exp(-jnp.abs(v)))


def _silu(v):
    return v * jax.nn.sigmoid(v)


def _split2(v):
    hi = v.astype(BF16)
    mid = (v - hi.astype(F32)).astype(BF16)
    return hi, mid


ROW_TILES = D_MODEL // LANES


def _store_token_tiles(ref, v, row0=0):
    r = v.shape[0]
    for s in range(ROW_TILES):
        ref[pl.ds(row0 * ROW_TILES + s, r, stride=ROW_TILES), :] = v[:, s * LANES:(s + 1) * LANES]


def _load_token_tiles(ref, r):
    return jnp.concatenate([ref[pl.ds(s, r, stride=ROW_TILES), :] for s in range(ROW_TILES)], axis=1)


def _transpose_rows(v):
    r = v.shape[0]
    if r % LANES:
        v = jnp.concatenate([v, jnp.zeros((LANES - r % LANES, v.shape[1]), v.dtype)], axis=0)
    return v.T[:, :r]


def _ssd_chunk(xs, bm, cm, dt, a_neg, z, dskip, s_scr, tril, e2):
    q = xs.shape[0]
    da = dt * a_neg
    d_hi = da.astype(BF16)
    r1 = da - d_hi.astype(F32)
    d_mid = r1.astype(BF16)
    d_lo = (r1 - d_mid.astype(F32)).astype(BF16)
    cs = jnp.dot(tril, jnp.concatenate([d_hi, d_mid, d_lo], axis=1), preferred_element_type=F32)
    acum = cs[:, 0:LANES] + cs[:, LANES:2 * LANES] + cs[:, 2 * LANES:3 * LANES]
    acum_t = _transpose_rows(acum)
    total = acum[q - 1:q, :]
    stack = jnp.concatenate(
        [dt, jnp.exp(total - acum), jnp.exp(acum), jnp.broadcast_to(jnp.exp(total), (8, LANES))], axis=0)
    s_hi, s_mid = _split2(stack)
    ex = jnp.dot(jnp.concatenate([s_hi, s_mid], axis=1), e2, preferred_element_type=F32)
    dt_e, dec_e, ea_e, cd_e = ex[0:q], ex[q:2 * q], ex[2 * q:3 * q], ex[3 * q:3 * q + 1]
    xdt = xs * dt_e
    xdt_b = xdt.astype(BF16)
    xdec_b = (xdt * dec_e).astype(BF16)
    row = lax.broadcasted_iota(jnp.int32, (q, q), 0)
    col = lax.broadcasted_iota(jnp.int32, (q, q), 1)
    causal = row >= col
    lane = lax.broadcasted_iota(jnp.int32, (q, LANES), 1)
    gw = D_SSM // SSM_GROUPS
    hpg = SSM_HEADS // SSM_GROUPS
    ys = []
    for g in range(SSM_GROUPS):
        cg = cm[:, g * SSM_STATE:(g + 1) * SSM_STATE].astype(BF16)
        bg = bm[:, g * SSM_STATE:(g + 1) * SSM_STATE].astype(BF16)
        cb = lax.dot_general(cg, bg, (((1,), (1,)), ((), ())), preferred_element_type=F32)
        s_g = s_scr[g]
        y_off = jnp.dot(cg, s_g.astype(BF16), preferred_element_type=F32) * ea_e[:, g * gw:(g + 1) * gw]
        pieces = []
        for j in range(hpg // 2):
            h0 = g * hpg + 2 * j
            ms = []
            for h in (h0, h0 + 1):
                diff = acum[:, h:h + 1] - acum_t[h:h + 1, :]
                ms.append(cb * jnp.exp(jnp.where(causal, diff, NEG_BIG)))
            m = jnp.concatenate(ms, axis=1).astype(BF16)
            xp = xdt_b[:, h0 * SSM_HEAD_DIM:(h0 + 2) * SSM_HEAD_DIM]
            zero = jnp.zeros_like(xp)
            rhs = jnp.concatenate([jnp.where(lane < SSM_HEAD_DIM, xp, zero),
                                   jnp.where(lane >= SSM_HEAD_DIM, xp, zero)], axis=0)
            pieces.append(jnp.dot(m, rhs, preferred_element_type=F32))
        ys.append(jnp.concatenate(pieces, axis=1) + y_off)
        upd = lax.dot_general(bg, xdec_b[:, g * gw:(g + 1) * gw], (((0,), (0,)), ((), ())),
                              preferred_element_type=F32)
        s_scr[g] = s_g * cd_e[:, g * gw:(g + 1) * gw] + upd
    y = jnp.concatenate(ys, axis=1) + dskip * xs
    return y * _silu(z)


def _mixer_kernel(*refs, tl, q, nl, n_alias):
    (x_ref, ca0_ref, cs0_ref, ssm0_ref, nmix_ref, win_ref, caw_ref, na_ref, csw_ref, csb_ref,
     dtb_ref, alog_ref, dsk_ref, ns_ref, wout_ref, nffn_ref, wr_ref, br_ref,
     tril_ref, e2_ref, g_ref, e3_ref) = refs[:22]
    refs = refs[22 + n_alias:]
    x1_ref, xf_ref, rrow_ref, route_ref, ca_ref, cs_ref, ssm_ref, upad, xpad, s_scr, ca_scr, cs_scr = refs
    l = pl.program_id(1)
    last = nl - 1
    pa_rows = CONV_A_WIDTH - 1
    ps_rows = SSM_CONV_WIDTH - 1

    @pl.when(l == 0)
    def _():
        ca_scr[...] = ca0_ref[...]
        cs_scr[...] = cs0_ref[...]
        s_scr[...] = ssm0_ref[...]

    caw = caw_ref[...]
    csw = csw_ref[...]
    a_neg = -jnp.exp(alog_ref[...])
    gam = jnp.concatenate([na_ref[...], ns_ref[...]], axis=1)
    nb = D_SSM + SSM_GROUPS * SSM_STATE
    lane = lax.broadcasted_iota(jnp.int32, (q, LANES), 1)
    lane_f = lane.astype(F32)
    tail_a = ca_scr[...]
    tail_s = cs_scr[...]
    for j in range(tl // q):
        rows = slice(j * q, (j + 1) * q)
        x = x_ref[rows, :]
        xn = (x * lax.rsqrt(jnp.mean(x * x, axis=-1, keepdims=True) + EPS) * nmix_ref[...]).astype(BF16)

        pa = jnp.dot(xn, win_ref[:, 0:COL_A], preferred_element_type=F32)
        a_b, a_c, a_h = pa[:, 0:D_CONV], pa[:, D_CONV:2 * D_CONV], pa[:, 2 * D_CONV:3 * D_CONV]
        u = a_c * a_h
        upad[j, 8 - pa_rows:8, :] = tail_a
        upad[j, 8:8 + q, :] = u
        conv = upad[j, 8 - pa_rows:8 - pa_rows + q, :] * caw[0:1, :]
        for k in range(1, CONV_A_WIDTH):
            conv = conv + upad[j, 8 - pa_rows + k:8 - pa_rows + k + q, :] * caw[k:k + 1, :]
        y_a = a_b * conv
        tail_a = u[q - pa_rows:q, :]

        ps = jnp.dot(xn, win_ref[:, COL_A:D_IN_PAD], preferred_element_type=F32)
        z = ps[:, 0:D_SSM]
        xbc = ps[:, D_SSM:D_SSM + D_XBC]
        xpad[j, 8 - ps_rows:8, :] = tail_s
        xpad[j, 8:8 + q, :] = xbc
        dt = _softplus(ps[:, D_SSM + D_XBC:COL_S] + dtb_ref[...])
        acc = xpad[j, 8 - ps_rows:8 - ps_rows + q, :] * csw[0:1, :]
        for k in range(1, SSM_CONV_WIDTH):
            acc = acc + xpad[j, 8 - ps_rows + k:8 - ps_rows + k + q, :] * csw[k:k + 1, :]
        xc = _silu(acc + csb_ref[...])
        tail_s = xbc[q - ps_rows:q, :]
        y_s = _ssd_chunk(xc[:, 0:D_SSM], xc[:, D_SSM:nb], xc[:, nb:D_XBC], dt, a_neg, z, dsk_ref[...],
                         s_scr, tril_ref[...], e2_ref[...])

        mixed = jnp.concatenate([y_a, y_s], axis=1)
        mean = jnp.dot((mixed * mixed).astype(BF16), g_ref[...], preferred_element_type=F32)
        r_hi, r_mid = _split2(lax.rsqrt(mean + EPS))
        rs_e = jnp.dot(jnp.concatenate([r_hi, r_mid], axis=1), e3_ref[...], preferred_element_type=F32)
        mixed_n = (mixed * rs_e * gam).astype(BF16)
        x1 = x + jnp.dot(mixed_n, wout_ref[...], preferred_element_type=F32)
        x1_ref[rows, :] = x1

        xf = x1 * lax.rsqrt(jnp.mean(x1 * x1, axis=-1, keepdims=True) + EPS) * nffn_ref[...]
        _store_token_tiles(xf_ref, xf, row0=j * q)
        f_hi, f_mid = _split2(xf)
        logits = jnp.dot(jnp.concatenate([f_hi, f_hi, f_mid], axis=1), wr_ref[...],
                         preferred_element_type=F32) + br_ref[...]
        work = logits
        vals, idxs = [], []
        for _ in range(TOP_K):
            m = jnp.max(work, axis=-1, keepdims=True)
            idx = jnp.min(jnp.where(work == m, lane_f, float(LANES)), axis=-1, keepdims=True)
            vals.append(m)
            idxs.append(idx)
            work = jnp.where(lane_f == idx, -jnp.inf, work)
        es = [jnp.exp(v - vals[0]) for v in vals]
        inv = 1.0 / (es[0] + es[1] + es[2] + es[3])
        route = jnp.zeros((q, LANES), F32)
        for k in range(TOP_K):
            route = jnp.where(lane == k, idxs[k], route)
            route = jnp.where(lane == TOP_K + k, es[k] * inv, route)
        rrow_ref[rows, :] = route
        route_ref[:, rows] = _transpose_rows(route)[0:2 * TOP_K, :]

    ca_scr[...] = tail_a
    cs_scr[...] = tail_s

    @pl.when(l == last)
    def _():
        ca_ref[...] = tail_a
        cs_ref[...] = tail_s
        ssm_ref[...] = s_scr[...]


def _const_spec(shape):
    nd = len(shape)
    return pl.BlockSpec(shape, lambda b, l: (0,) * nd, pipeline_mode=pl.Buffered(1))


def _mixer_call(x, ca0, cs0, ssm0, wts, consts, *, tl, q, t_all, row_off, alias=None):
    b_sz, l_sz, _ = x.shape
    nl = l_sz // tl
    blk_off = row_off // tl
    n_alias = 0 if alias is None else len(alias)
    in_specs = [
        pl.BlockSpec((None, tl, D_MODEL), lambda b, l: (b, l, 0)),
        pl.BlockSpec((None, CONV_A_WIDTH - 1, D_CONV), lambda b, l: (b, 0, 0)),
        pl.BlockSpec((None, SSM_CONV_WIDTH - 1, D_XBC), lambda b, l: (b, 0, 0)),
        pl.BlockSpec((None, SSM_GROUPS, SSM_STATE, D_SSM // SSM_GROUPS), lambda b, l: (b, 0, 0, 0)),
    ] + [_const_spec(w.shape) for w in wts] + [_const_spec(c.shape) for c in consts]
    in_specs += [pl.BlockSpec(memory_space=pl.ANY)] * n_alias
    row_spec = pl.BlockSpec((tl, D_MODEL), lambda b, l: (blk_off + b * nl + l, 0))
    out_specs = [
        row_spec,
        pl.BlockSpec((tl * ROW_TILES, LANES), lambda b, l: (blk_off + b * nl + l, 0)),
        pl.BlockSpec((tl, LANES), lambda b, l: (blk_off + b * nl + l, 0)),
        pl.BlockSpec((None, 2 * TOP_K, tl), lambda b, l: (b, 0, l)),
        pl.BlockSpec((None, CONV_A_WIDTH - 1, D_CONV), lambda b, l: (b, 0, 0)),
        pl.BlockSpec((None, SSM_CONV_WIDTH - 1, D_XBC), lambda b, l: (b, 0, 0)),
        pl.BlockSpec((None, SSM_GROUPS, SSM_STATE, D_SSM // SSM_GROUPS), lambda b, l: (b, 0, 0, 0)),
    ]
    out_shape = [
        jax.ShapeDtypeStruct((t_all, D_MODEL), F32),
        jax.ShapeDtypeStruct((t_all * ROW_TILES, LANES), F32),
        jax.ShapeDtypeStruct((t_all, LANES), F32),
        jax.ShapeDtypeStruct((b_sz, 2 * TOP_K, l_sz), F32),
        jax.ShapeDtypeStruct((b_sz, CONV_A_WIDTH - 1, D_CONV), F32),
        jax.ShapeDtypeStruct((b_sz, SSM_CONV_WIDTH - 1, D_XBC), F32),
        jax.ShapeDtypeStruct((b_sz, SSM_GROUPS, SSM_STATE, D_SSM // SSM_GROUPS), F32),
    ]
    n_in = 4 + len(wts) + len(consts)
    aliases = {n_in + i: i for i in range(n_alias)}
    args = [x, ca0, cs0, ssm0, *wts, *consts] + ([] if alias is None else list(alias))
    return pl.pallas_call(
        functools.partial(_mixer_kernel, tl=tl, q=q, nl=nl, n_alias=n_alias),
        grid=(b_sz, nl),
        in_specs=in_specs,
        out_specs=out_specs,
        out_shape=out_shape,
        scratch_shapes=[
            pltpu.VMEM((tl // q, q + 8, D_CONV), F32),
            pltpu.VMEM((tl // q, q + 8, D_XBC), F32),
            pltpu.VMEM((SSM_GROUPS, SSM_STATE, D_SSM // SSM_GROUPS), F32),
            pltpu.VMEM((CONV_A_WIDTH - 1, D_CONV), F32),
            pltpu.VMEM((SSM_CONV_WIDTH - 1, D_XBC), F32),
        ],
        input_output_aliases=aliases,
        compiler_params=pltpu.CompilerParams(
            dimension_semantics=("arbitrary", "arbitrary"), vmem_limit_bytes=VMEM_LIMIT),
        name="mixer",
    )(*args)


def _mixer_consts(q):
    r = jnp.arange(q)
    tril = (r[:, None] >= r[None, :]).astype(BF16)
    j2 = jnp.arange(2 * LANES) % LANES
    e2 = (j2[:, None] == (jnp.arange(D_SSM) // SSM_HEAD_DIM)[None, :]).astype(BF16)
    c = jnp.arange(D_CONV + D_SSM)
    grp = jnp.where(c < D_CONV, c // (D_CONV // CONV_GROUPS),
                    CONV_GROUPS + (c - D_CONV) // (D_SSM // SSM_GROUPS))
    wgt = jnp.where(c < D_CONV, CONV_GROUPS / D_CONV, SSM_GROUPS / D_SSM)
    g = jnp.where(grp[:, None] == jnp.arange(LANES)[None, :], wgt[:, None], 0.0).astype(BF16)
    e3 = (j2[:, None] == grp[None, :]).astype(BF16)
    return tril, e2, g, e3


def _prep_mixer_weights(norm_mix, w_in, conv_a_w, norm_conv_a, conv_ssm_w, conv_ssm_b, dt_bias, a_log,
                        d_skip, norm_ssm, w_out, norm_ffn, w_router, b_router):
    hpad = LANES - SSM_HEADS
    w_in_p = jnp.pad(w_in[0], ((0, 0), (0, hpad))).astype(BF16)
    wr = jnp.pad(w_router[0], ((0, 0), (0, LANES - N_EXPERTS)))
    wr_hi = wr.astype(BF16)
    wr_mid = (wr - wr_hi.astype(F32)).astype(BF16)
    wr3 = jnp.concatenate([wr_hi, wr_mid, wr_hi], axis=0)
    br = jnp.pad(b_router[0], (0, LANES - N_EXPERTS), constant_values=NEG_BIG)[None]
    return (norm_mix[0][None], w_in_p, conv_a_w[0], norm_conv_a[0][None], conv_ssm_w[0],
            conv_ssm_b[0][None], jnp.pad(dt_bias[0], (0, hpad))[None], jnp.pad(a_log[0], (0, hpad))[None],
            jnp.repeat(d_skip[0], SSM_HEAD_DIM)[None], norm_ssm[0][None], w_out[0].astype(BF16),
            norm_ffn[0][None], wr3, br)


def _ssm_to_internal(s):
    b = s.shape[0]
    s = s.reshape(b, SSM_GROUPS, SSM_HEADS // SSM_GROUPS, SSM_HEAD_DIM, SSM_STATE)
    return s.transpose(0, 1, 4, 2, 3).reshape(b, SSM_GROUPS, SSM_STATE, D_SSM // SSM_GROUPS)


def _ssm_from_internal(s):
    b = s.shape[0]
    s = s.reshape(b, SSM_GROUPS, SSM_STATE, SSM_HEADS // SSM_GROUPS, SSM_HEAD_DIM)
    return s.transpose(0, 1, 3, 4, 2).reshape(b, SSM_HEADS, SSM_HEAD_DIM, SSM_STATE)


MIX_TL = 512
MIX_Q = 256
COMB_TT = 256


def _routing(top_i):
    t = top_i.shape[0]
    n = t * TOP_K
    assert n % MOE_BLOCK == 0
    nblk = n // MOE_BLOCK
    n_items = nblk + N_EXPERTS
    shift = max(n - 1, 1).bit_length()
    assert N_EXPERTS << shift < 2 ** 31
    flat_e = top_i.reshape(n)
    ar = jnp.arange(n, dtype=jnp.int32)
    key = lax.sort(flat_e * (1 << shift) + ar)
    sorted_e = key >> shift
    slot_tok = (key & ((1 << shift) - 1)) // TOP_K
    experts = jnp.arange(N_EXPERTS, dtype=jnp.int32)
    group_start = jnp.sum((sorted_e[:, None] < experts[None, :]).astype(jnp.int32), axis=0)
    onehot = (flat_e[:, None] == experts[None, :]).astype(jnp.int32)
    rank = jnp.take_along_axis(jnp.cumsum(onehot, axis=0), flat_e[:, None], axis=1)[:, 0] - 1
    pos = group_start[flat_e] + rank
    bp = lax.sort(jnp.concatenate([jnp.arange(nblk, dtype=jnp.int32) * MOE_BLOCK, group_start]))
    dup = jnp.concatenate([jnp.zeros((1,), bool), bp[1:] == bp[:-1]])
    bp = lax.sort(jnp.where(jnp.logical_or(dup, bp >= n), n, bp))
    blk = jnp.minimum(bp // MOE_BLOCK, nblk - 1)
    lo = bp - blk * MOE_BLOCK
    exp = sorted_e[jnp.minimum(bp, n - 1)]
    items = tuple(a.astype(jnp.int32) for a in (blk, exp, lo))
    assert items[0].shape == (n_items,)
    return slot_tok.astype(jnp.int32).reshape(nblk, 1, MOE_BLOCK), pos.astype(jnp.int32), items


def _tile_gather_start(src_hbm, idx_ref, idx_off, n_rows, dst, sem):
    def body(i, carry):
        src_row = pl.multiple_of(idx_ref[0, idx_off + i] * ROW_TILES, ROW_TILES)
        dst_row = pl.multiple_of(i * ROW_TILES, ROW_TILES)
        pltpu.make_async_copy(src_hbm.at[pl.ds(src_row, ROW_TILES), :],
                              dst.at[pl.ds(dst_row, ROW_TILES), :], sem).start()
        return carry
    lax.fori_loop(0, n_rows, body, 0, unroll=8)


def _tile_gather_start_inline(src_hbm, idx_ref, idx_off, n_rows, dst, sem):
    for i in range(n_rows):
        src_row = pl.multiple_of(idx_ref[0, idx_off + i] * ROW_TILES, ROW_TILES)
        pltpu.make_async_copy(src_hbm.at[pl.ds(src_row, ROW_TILES), :],
                              dst.at[pl.ds(i * ROW_TILES, ROW_TILES), :], sem).start()


def _expert_kernel(blk_ref, exp_ref, lo_ref, tok_cur, tok_nxt, xf_hbm,
                   wg_ref, bg_ref, wu_ref, bu_ref, wd_ref, bd_ref, yb_ref, xbuf, sem, wgb, wub, wdb, *, n_items):
    w = pl.program_id(0)
    lo = lo_ref[w]
    slot = w % 2

    def wait_rows(s):
        pltpu.make_async_copy(xf_hbm.at[pl.ds(0, MOE_BLOCK * ROW_TILES), :], xbuf.at[s], sem.at[s]).wait()

    @pl.when(w == 0)
    def _():
        _tile_gather_start(xf_hbm, tok_cur, 0, MOE_BLOCK, xbuf.at[0], sem.at[0])

    @pl.when(jnp.logical_or(w == 0, exp_ref[w] != exp_ref[jnp.maximum(w - 1, 0)]))
    def _():
        wgb[...] = wg_ref[...].astype(BF16)
        wub[...] = wu_ref[...].astype(BF16)
        wdb[...] = wd_ref[...].astype(BF16)

    wait_rows(slot)
    xb = _load_token_tiles(xbuf.at[slot], MOE_BLOCK).astype(BF16)
    gate = jnp.dot(xb, wgb[...], preferred_element_type=F32) + bg_ref[...]
    up = jnp.dot(xb, wub[...], preferred_element_type=F32) + bu_ref[...]
    gate = jnp.minimum(gate, SWIGLU_LIMIT)
    up = jnp.clip(up, -SWIGLU_LIMIT, SWIGLU_LIMIT)
    glu = gate * jax.nn.sigmoid(gate * SWIGLU_ALPHA)
    h = ((up + 1.0) * glu).astype(BF16)
    y = jnp.dot(h, wdb[...], preferred_element_type=F32) + bd_ref[...]
    keep = lax.broadcasted_iota(jnp.int32, (MOE_BLOCK, D_MODEL), 0) < lo
    _store_token_tiles(yb_ref, jnp.where(keep, _load_token_tiles(yb_ref, MOE_BLOCK), y))
    _tile_gather_start_inline(xf_hbm, tok_nxt, 0, MOE_BLOCK, xbuf.at[1 - slot], sem.at[1 - slot])

    @pl.when(w == n_items - 1)
    def _():
        wait_rows(1 - slot)


def _expert_call(slot_tok, items, xf, w_gate, b_gate, w_up, b_up, w_down, b_down):
    nblk = slot_tok.shape[0]
    n_items = items[0].shape[0]
    d, de = w_gate.shape[1], w_gate.shape[2]
    tok_spec = lambda f: pl.BlockSpec((None, 1, MOE_BLOCK), f, memory_space=pltpu.SMEM)
    w_spec = lambda r, c: pl.BlockSpec((None, r, c), lambda w, blk, exp, lo: (exp[w], 0, 0))
    grid_spec = pltpu.PrefetchScalarGridSpec(
        num_scalar_prefetch=3,
        grid=(n_items,),
        in_specs=[
            tok_spec(lambda w, blk, exp, lo: (blk[w], 0, 0)),
            tok_spec(lambda w, blk, exp, lo: (blk[jnp.minimum(w + 1, n_items - 1)], 0, 0)),
            pl.BlockSpec(memory_space=pl.ANY),
            w_spec(d, de), w_spec(1, de), w_spec(d, de), w_spec(1, de), w_spec(de, d), w_spec(1, d),
        ],
        out_specs=pl.BlockSpec((MOE_BLOCK * ROW_TILES, LANES), lambda w, blk, exp, lo: (blk[w], 0)),
        scratch_shapes=[
            pltpu.VMEM((2, MOE_BLOCK * ROW_TILES, LANES), F32),
            pltpu.SemaphoreType.DMA((2,)),
            pltpu.VMEM((d, de), BF16), pltpu.VMEM((d, de), BF16), pltpu.VMEM((de, d), BF16),
        ],
    )
    return pl.pallas_call(
        functools.partial(_expert_kernel, n_items=n_items),
        grid_spec=grid_spec,
        out_shape=jax.ShapeDtypeStruct((nblk * MOE_BLOCK * ROW_TILES, LANES), F32),
        compiler_params=pltpu.CompilerParams(dimension_semantics=("arbitrary",), vmem_limit_bytes=VMEM_LIMIT),
        name="experts",
    )(*items, slot_tok, slot_tok, xf, w_gate, b_gate[:, None, :], w_up, b_up[:, None, :],
      w_down, b_down[:, None, :])


def _combine_kernel(pos_cur, pos_nxt, x1_ref, rrow_ref, yb_hbm, fn_ref, out_ref, ybuf, sem, *, n):
    i = pl.program_id(0)
    slot = i % 2
    tt = x1_ref.shape[0]

    def wait_rows(s):
        for k in range(TOP_K):
            pltpu.make_async_copy(yb_hbm.at[pl.ds(0, tt * ROW_TILES), :], ybuf.at[s, k], sem.at[s]).wait()

    @pl.when(i == 0)
    def _():
        for k in range(TOP_K):
            _tile_gather_start(yb_hbm, pos_cur, k * tt, tt, ybuf.at[0, k], sem.at[0])

    wait_rows(slot)
    rr = rrow_ref[...]
    acc = x1_ref[...]
    for k in range(TOP_K):
        acc = acc + rr[:, TOP_K + k:TOP_K + k + 1] * _load_token_tiles(ybuf.at[slot, k], tt)
    out_ref[...] = acc * lax.rsqrt(jnp.mean(acc * acc, axis=-1, keepdims=True) + EPS) * fn_ref[...]
    for k in range(TOP_K):
        _tile_gather_start_inline(yb_hbm, pos_nxt, k * tt, tt, ybuf.at[1 - slot, k], sem.at[1 - slot])

    @pl.when(i == n - 1)
    def _():
        wait_rows(1 - slot)


def _combine_call(pos, x1, rrow, yb, fn, *, row_off, n_rows):
    tt = min(COMB_TT, n_rows)
    n_tiles = n_rows // tt
    off = row_off // tt
    d = x1.shape[1]
    pos = pos[row_off * TOP_K:(row_off + n_rows) * TOP_K].reshape(n_tiles, tt, TOP_K)
    pos = pos.transpose(0, 2, 1).reshape(n_tiles, 1, TOP_K * tt)
    pos_spec = lambda f: pl.BlockSpec((None, 1, tt * TOP_K), f, memory_space=pltpu.SMEM)
    return pl.pallas_call(
        functools.partial(_combine_kernel, n=n_tiles),
        grid=(n_tiles,),
        in_specs=[
            pos_spec(lambda i: (i, 0, 0)),
            pos_spec(lambda i: (jnp.minimum(i + 1, n_tiles - 1), 0, 0)),
            pl.BlockSpec((tt, d), lambda i: (off + i, 0)),
            pl.BlockSpec((tt, LANES), lambda i: (off + i, 0)),
            pl.BlockSpec(memory_space=pl.ANY),
            pl.BlockSpec((1, d), lambda i: (0, 0)),
        ],
        out_specs=pl.BlockSpec((tt, d), lambda i: (i, 0)),
        out_shape=jax.ShapeDtypeStruct((n_rows, d), F32),
        scratch_shapes=[pltpu.VMEM((2, TOP_K, tt * ROW_TILES, LANES), F32), pltpu.SemaphoreType.DMA((2,))],
        compiler_params=pltpu.CompilerParams(dimension_semantics=("arbitrary",), vmem_limit_bytes=VMEM_LIMIT),
        name="combine",
    )(pos, pos, x1, rrow, yb, fn)


def kernel(x_prompt, x_sample, cache_conv_a, cache_conv_ssm, state_ssm, norm_mix, w_in, conv_a_w, norm_conv_a, conv_ssm_w, conv_ssm_b, dt_bias, a_log, d_skip, norm_ssm, w_out, norm_ffn, w_router, b_router, w_gate, b_gate, w_up, b_up, w_down, b_down, final_norm):
    bp, lp, _ = x_prompt.shape
    bs, ls, _ = x_sample.shape
    t_p, t_s = bp * lp, bs * ls
    t_all = t_p + t_s
    wts = _prep_mixer_weights(norm_mix, w_in, conv_a_w, norm_conv_a, conv_ssm_w, conv_ssm_b, dt_bias, a_log,
                              d_skip, norm_ssm, w_out, norm_ffn, w_router, b_router)
    zeros = lambda *s: jnp.zeros(s, F32)
    x1, xf, rrow, route_p, ca_p, cs_p, st_p = _mixer_call(
        x_prompt, zeros(bp, CONV_A_WIDTH - 1, D_CONV), zeros(bp, SSM_CONV_WIDTH - 1, D_XBC),
        zeros(bp, SSM_GROUPS, SSM_STATE, D_SSM // SSM_GROUPS), wts, _mixer_consts(MIX_Q),
        tl=MIX_TL, q=MIX_Q, t_all=t_all, row_off=0)
    x1, xf, rrow, route_s, ca_s, cs_s, st_s = _mixer_call(
        x_sample, cache_conv_a[0], cache_conv_ssm[0], _ssm_to_internal(state_ssm[0]), wts, _mixer_consts(ls),
        tl=ls, q=ls, t_all=t_all, row_off=t_p, alias=(x1, xf, rrow))

    top_i = jnp.concatenate([route_p[:, 0:TOP_K, :].transpose(0, 2, 1).reshape(t_p, TOP_K),
                             route_s[:, 0:TOP_K, :].transpose(0, 2, 1).reshape(t_s, TOP_K)], axis=0)
    slot_tok, pos, items = _routing(top_i.astype(jnp.int32))

    yb = _expert_call(slot_tok, items, xf, w_gate[0], b_gate[0], w_up[0], b_up[0], w_down[0], b_down[0])
    fn = final_norm[None]
    y_p = _combine_call(pos, x1, rrow, yb, fn, row_off=0, n_rows=t_p)
    y_s = _combine_call(pos, x1, rrow, yb, fn, row_off=t_p, n_rows=t_s)
    return (y_p.reshape(bp, lp, D_MODEL), y_s.reshape(bs, ls, D_MODEL),
            ca_p[None], cs_p[None], _ssm_from_internal(st_p)[None],
            ca_s[None], cs_s[None], _ssm_from_internal(st_s)[None])
```

```python
import functools

import jax
import jax.numpy as jnp
from jax import lax
from jax.experimental import pallas as pl
from jax.experimental.pallas import tpu as pltpu

F32 = jnp.float32
BF16 = jnp.bfloat16

D_MODEL = 1024
D_CONV = 1024
D_SSM = 1024
CONV_GROUPS = 16
CONV_A_WIDTH = 3
SSM_HEAD_DIM = 64
SSM_HEADS = 16
SSM_GROUPS = 2
SSM_STATE = 128
SSM_CONV_WIDTH = 4
D_XBC = D_SSM + 2 * SSM_GROUPS * SSM_STATE
N_EXPERTS = 32
TOP_K = 4
SWIGLU_LIMIT = 7.0
SWIGLU_ALPHA = 1.702
MOE_BLOCK = 256
EPS = 1e-5

LANES = 128
COL_A = 3 * D_CONV
COL_S = D_SSM + D_XBC + LANES
D_IN_PAD = COL_A + COL_S
NEG_BIG = -1e30
VMEM_LIMIT = 56 * 1024 * 1024


def _softplus(v):
    return jnp.maximum(v, 0.0) + jnp.log1p(jnp.exp(-jnp.abs(v)))


def _silu(v):
    return v * jax.nn.sigmoid(v)


def _split2(v):
    hi = v.astype(BF16)
    mid = (v - hi.astype(F32)).astype(BF16)
    return hi, mid


ROW_TILES = D_MODEL // LANES


def _store_token_tiles(ref, v, row0=0):
    r = v.shape[0]
    for s in range(ROW_TILES):
        ref[pl.ds(row0 * ROW_TILES + s, r, stride=ROW_TILES), :] = v[:, s * LANES:(s + 1) * LANES]


def _load_token_tiles(ref, r):
    return jnp.concatenate([ref[pl.ds(s, r, stride=ROW_TILES), :] for s in range(ROW_TILES)], axis=1)


def _transpose_rows(v):
    r = v.shape[0]
    if r % LANES:
        v = jnp.concatenate([v, jnp.zeros((LANES - r % LANES, v.shape[1]), v.dtype)], axis=0)
    return v.T[:, :r]


def _ssd_chunk(xs, bm, cm, dt, a_neg, z, dskip, s_scr, tril, e2):
    q = xs.shape[0]
    da = dt * a_neg
    d_hi = da.astype(BF16)
    r1 = da - d_hi.astype(F32)
    d_mid = r1.astype(BF16)
    d_lo = (r1 - d_mid.astype(F32)).astype(BF16)
    cs = jnp.dot(tril, jnp.concatenate([d_hi, d_mid, d_lo], axis=1), preferred_element_type=F32)
    acum = cs[:, 0:LANES] + cs[:, LANES:2 * LANES] + cs[:, 2 * LANES:3 * LANES]
    acum_t = _transpose_rows(acum)
    total = acum[q - 1:q, :]
    stack = jnp.concatenate(
        [dt, jnp.exp(total - acum), jnp.exp(acum), jnp.broadcast_to(jnp.exp(total), (8, LANES))], axis=0)
    s_hi, s_mid = _split2(stack)
    ex = jnp.dot(jnp.concatenate([s_hi, s_mid], axis=1), e2, preferred_element_type=F32)
    dt_e, dec_e, ea_e, cd_e = ex[0:q], ex[q:2 * q], ex[2 * q:3 * q], ex[3 * q:3 * q + 1]
    xdt = xs * dt_e
    xdt_b = xdt.astype(BF16)
    xdec_b = (xdt * dec_e).astype(BF16)
    row = lax.broadcasted_iota(jnp.int32, (q, q), 0)
    col = lax.broadcasted_iota(jnp.int32, (q, q), 1)
    causal = row >= col
    lane = lax.broadcasted_iota(jnp.int32, (q, LANES), 1)
    gw = D_SSM // SSM_GROUPS
    hpg = SSM_HEADS // SSM_GROUPS
    ys = []
    for g in range(SSM_GROUPS):
        cg = cm[:, g * SSM_STATE:(g + 1) * SSM_STATE].astype(BF16)
        bg = bm[:, g * SSM_STATE:(g + 1) * SSM_STATE].astype(BF16)
        cb = lax.dot_general(cg, bg, (((1,), (1,)), ((), ())), preferred_element_type=F32)
        s_g = s_scr[g]
        y_off = jnp.dot(cg, s_g.astype(BF16), preferred_element_type=F32) * ea_e[:, g * gw:(g + 1) * gw]
        pieces = []
        for j in range(hpg // 2):
            h0 = g * hpg + 2 * j
            ms = []
            for h in (h0, h0 + 1):
                diff = acum[:, h:h + 1] - acum_t[h:h + 1, :]
                ms.append(cb * jnp.exp(jnp.where(causal, diff, NEG_BIG)))
            m = jnp.concatenate(ms, axis=1).astype(BF16)
            xp = xdt_b[:, h0 * SSM_HEAD_DIM:(h0 + 2) * SSM_HEAD_DIM]
            zero = jnp.zeros_like(xp)
            rhs = jnp.concatenate([jnp.where(lane < SSM_HEAD_DIM, xp, zero),
                                   jnp.where(lane >= SSM_HEAD_DIM, xp, zero)], axis=0)
            pieces.append(jnp.dot(m, rhs, preferred_element_type=F32))
        ys.append(jnp.concatenate(pieces, axis=1) + y_off)
        upd = lax.dot_general(bg, xdec_b[:, g * gw:(g + 1) * gw], (((0,), (0,)), ((), ())),
                              preferred_element_type=F32)
        s_scr[g] = s_g * cd_e[:, g * gw:(g + 1) * gw] + upd
    y = jnp.concatenate(ys, axis=1) + dskip * xs
    return y * _silu(z)


def _mixer_kernel(*refs, tl, q, nl, n_alias):
    (x_ref, ca0_ref, cs0_ref, ssm0_ref, nmix_ref, win_ref, caw_ref, na_ref, csw_ref, csb_ref,
     dtb_ref, alog_ref, dsk_ref, ns_ref, wout_ref, nffn_ref, wr_ref, br_ref,
     tril_ref, e2_ref, g_ref, e3_ref) = refs[:22]
    refs = refs[22 + n_alias:]
    x1_ref, xf_ref, rrow_ref, route_ref, ca_ref, cs_ref, ssm_ref, upad, xpad, s_scr, ca_scr, cs_scr = refs
    l = pl.program_id(1)
    last = nl - 1
    pa_rows = CONV_A_WIDTH - 1
    ps_rows = SSM_CONV_WIDTH - 1

    @pl.when(l == 0)
    def _():
        ca_scr[...] = ca0_ref[...]
        cs_scr[...] = cs0_ref[...]
        s_scr[...] = ssm0_ref[...]

    caw = caw_ref[...]
    csw = csw_ref[...]
    a_neg = -jnp.exp(alog_ref[...])
    gam = jnp.concatenate([na_ref[...], ns_ref[...]], axis=1)
    nb = D_SSM + SSM_GROUPS * SSM_STATE
    lane = lax.broadcasted_iota(jnp.int32, (q, LANES), 1)
    lane_f = lane.astype(F32)
    tail_a = ca_scr[...]
    tail_s = cs_scr[...]
    for j in range(tl // q):
        rows = slice(j * q, (j + 1) * q)
        x = x_ref[rows, :]
        xn = (x * lax.rsqrt(jnp.mean(x * x, axis=-1, keepdims=True) + EPS) * nmix_ref[...]).astype(BF16)

        pa = jnp.dot(xn, win_ref[:, 0:COL_A], preferred_element_type=F32)
        a_b, a_c, a_h = pa[:, 0:D_CONV], pa[:, D_CONV:2 * D_CONV], pa[:, 2 * D_CONV:3 * D_CONV]
        u = a_c * a_h
        upad[j, 8 - pa_rows:8, :] = tail_a
        upad[j, 8:8 + q, :] = u
        conv = upad[j, 8 - pa_rows:8 - pa_rows + q, :] * caw[0:1, :]
        for k in range(1, CONV_A_WIDTH):
            conv = conv + upad[j, 8 - pa_rows + k:8 - pa_rows + k + q, :] * caw[k:k + 1, :]
        y_a = a_b * conv
        tail_a = u[q - pa_rows:q, :]

        ps = jnp.dot(xn, win_ref[:, COL_A:D_IN_PAD], preferred_element_type=F32)
        z = ps[:, 0:D_SSM]
        xbc = ps[:, D_SSM:D_SSM + D_XBC]
        xpad[j, 8 - ps_rows:8, :] = tail_s
        xpad[j, 8:8 + q, :] = xbc
        dt = _softplus(ps[:, D_SSM + D_XBC:COL_S] + dtb_ref[...])
        acc = xpad[j, 8 - ps_rows:8 - ps_rows + q, :] * csw[0:1, :]
        for k in range(1, SSM_CONV_WIDTH):
            acc = acc + xpad[j, 8 - ps_rows + k:8 - ps_rows + k + q, :] * csw[k:k + 1, :]
        xc = _silu(acc + csb_ref[...])
        tail_s = xbc[q - ps_rows:q, :]
        y_s = _ssd_chunk(xc[:, 0:D_SSM], xc[:, D_SSM:nb], xc[:, nb:D_XBC], dt, a_neg, z, dsk_ref[...],
                         s_scr, tril_ref[...], e2_ref[...])

        mixed = jnp.concatenate([y_a, y_s], axis=1)
        mean = jnp.dot((mixed * mixed).astype(BF16), g_ref[...], preferred_element_type=F32)
        r_hi, r_mid = _split2(lax.rsqrt(mean + EPS))
        rs_e = jnp.dot(jnp.concatenate([r_hi, r_mid], axis=1), e3_ref[...], preferred_element_type=F32)
        mixed_n = (mixed * rs_e * gam).astype(BF16)
        x1 = x + jnp.dot(mixed_n, wout_ref[...], preferred_element_type=F32)
        x1_ref[rows, :] = x1

        xf = x1 * lax.rsqrt(jnp.mean(x1 * x1, axis=-1, keepdims=True) + EPS) * nffn_ref[...]
        _store_token_tiles(xf_ref, xf, row0=j * q)
        f_hi, f_mid = _split2(xf)
        logits = jnp.dot(jnp.concatenate([f_hi, f_hi, f_mid], axis=1), wr_ref[...],
                         preferred_element_type=F32) + br_ref[...]
        work = logits
        vals, idxs = [], []
        for _ in range(TOP_K):
            m = jnp.max(work, axis=-1, keepdims=True)
            idx = jnp.min(jnp.where(work == m, lane_f, float(LANES)), axis=-1, keepdims=True)
            vals.append(m)
            idxs.append(idx)
            work = jnp.where(lane_f == idx, -jnp.inf, work)
        es = [jnp.exp(v - vals[0]) for v in vals]
        inv = 1.0 / (es[0] + es[1] + es[2] + es[3])
        route = jnp.zeros((q, LANES), F32)
        for k in range(TOP_K):
            route = jnp.where(lane == k, idxs[k], route)
            route = jnp.where(lane == TOP_K + k, es[k] * inv, route)
        rrow_ref[rows, :] = route
        route_ref[:, rows] = _transpose_rows(route)[0:2 * TOP_K, :]

    ca_scr[...] = tail_a
    cs_scr[...] = tail_s

    @pl.when(l == last)
    def _():
        ca_ref[...] = tail_a
        cs_ref[...] = tail_s
        ssm_ref[...] = s_scr[...]


def _const_spec(shape):
    nd = len(shape)
    return pl.BlockSpec(shape, lambda b, l: (0,) * nd, pipeline_mode=pl.Buffered(1))


def _mixer_call(x, ca0, cs0, ssm0, wts, consts, *, tl, q, t_all, row_off, alias=None):
    b_sz, l_sz, _ = x.shape
    nl = l_sz // tl
    blk_off = row_off // tl
    n_alias = 0 if alias is None else len(alias)
    in_specs = [
        pl.BlockSpec((None, tl, D_MODEL), lambda b, l: (b, l, 0)),
        pl.BlockSpec((None, CONV_A_WIDTH - 1, D_CONV), lambda b, l: (b, 0, 0)),
        pl.BlockSpec((None, SSM_CONV_WIDTH - 1, D_XBC), lambda b, l: (b, 0, 0)),
        pl.BlockSpec((None, SSM_GROUPS, SSM_STATE, D_SSM // SSM_GROUPS), lambda b, l: (b, 0, 0, 0)),
    ] + [_const_spec(w.shape) for w in wts] + [_const_spec(c.shape) for c in consts]
    in_specs += [pl.BlockSpec(memory_space=pl.ANY)] * n_alias
    row_spec = pl.BlockSpec((tl, D_MODEL), lambda b, l: (blk_off + b * nl + l, 0))
    out_specs = [
        row_spec,
        pl.BlockSpec((tl * ROW_TILES, LANES), lambda b, l: (blk_off + b * nl + l, 0)),
        pl.BlockSpec((tl, LANES), lambda b, l: (blk_off + b * nl + l, 0)),
        pl.BlockSpec((None, 2 * TOP_K, tl), lambda b, l: (b, 0, l)),
        pl.BlockSpec((None, CONV_A_WIDTH - 1, D_CONV), lambda b, l: (b, 0, 0)),
        pl.BlockSpec((None, SSM_CONV_WIDTH - 1, D_XBC), lambda b, l: (b, 0, 0)),
        pl.BlockSpec((None, SSM_GROUPS, SSM_STATE, D_SSM // SSM_GROUPS), lambda b, l: (b, 0, 0, 0)),
    ]
    out_shape = [
        jax.ShapeDtypeStruct((t_all, D_MODEL), F32),
        jax.ShapeDtypeStruct((t_all * ROW_TILES, LANES), F32),
        jax.ShapeDtypeStruct((t_all, LANES), F32),
        jax.ShapeDtypeStruct((b_sz, 2 * TOP_K, l_sz), F32),
        jax.ShapeDtypeStruct((b_sz, CONV_A_WIDTH - 1, D_CONV), F32),
        jax.ShapeDtypeStruct((b_sz, SSM_CONV_WIDTH - 1, D_XBC), F32),
        jax.ShapeDtypeStruct((b_sz, SSM_GROUPS, SSM_STATE, D_SSM // SSM_GROUPS), F32),
    ]
    n_in = 4 + len(wts) + len(consts)
    aliases = {n_in + i: i for i in range(n_alias)}
    args = [x, ca0, cs0, ssm0, *wts, *consts] + ([] if alias is None else list(alias))
    return pl.pallas_call(
        functools.partial(_mixer_kernel, tl=tl, q=q, nl=nl, n_alias=n_alias),
        grid=(b_sz, nl),
        in_specs=in_specs,
        out_specs=out_specs,
        out_shape=out_shape,
        scratch_shapes=[
            pltpu.VMEM((tl // q, q + 8, D_CONV), F32),
            pltpu.VMEM((tl // q, q + 8, D_XBC), F32),
            pltpu.VMEM((SSM_GROUPS, SSM_STATE, D_SSM // SSM_GROUPS), F32),
            pltpu.VMEM((CONV_A_WIDTH - 1, D_CONV), F32),
            pltpu.VMEM((SSM_CONV_WIDTH - 1, D_XBC), F32),
        ],
        input_output_aliases=aliases,
        compiler_params=pltpu.CompilerParams(
            dimension_semantics=("arbitrary", "arbitrary"), vmem_limit_bytes=VMEM_LIMIT),
        name="mixer",
    )(*args)


def _mixer_consts(q):
    r = jnp.arange(q)
    tril = (r[:, None] >= r[None, :]).astype(BF16)
    j2 = jnp.arange(2 * LANES) % LANES
    e2 = (j2[:, None] == (jnp.arange(D_SSM) // SSM_HEAD_DIM)[None, :]).astype(BF16)
    c = jnp.arange(D_CONV + D_SSM)
    grp = jnp.where(c < D_CONV, c // (D_CONV // CONV_GROUPS),
                    CONV_GROUPS + (c - D_CONV) // (D_SSM // SSM_GROUPS))
    wgt = jnp.where(c < D_CONV, CONV_GROUPS / D_CONV, SSM_GROUPS / D_SSM)
    g = jnp.where(grp[:, None] == jnp.arange(LANES)[None, :], wgt[:, None], 0.0).astype(BF16)
    e3 = (j2[:, None] == grp[None, :]).astype(BF16)
    return tril, e2, g, e3


def _prep_mixer_weights(norm_mix, w_in, conv_a_w, norm_conv_a, conv_ssm_w, conv_ssm_b, dt_bias, a_log,
                        d_skip, norm_ssm, w_out, norm_ffn, w_router, b_router):
    hpad = LANES - SSM_HEADS
    w_in_p = jnp.pad(w_in[0], ((0, 0), (0, hpad))).astype(BF16)
    wr = jnp.pad(w_router[0], ((0, 0), (0, LANES - N_EXPERTS)))
    wr_hi = wr.astype(BF16)
    wr_mid = (wr - wr_hi.astype(F32)).astype(BF16)
    wr3 = jnp.concatenate([wr_hi, wr_mid, wr_hi], axis=0)
    br = jnp.pad(b_router[0], (0, LANES - N_EXPERTS), constant_values=NEG_BIG)[None]
    return (norm_mix[0][None], w_in_p, conv_a_w[0], norm_conv_a[0][None], conv_ssm_w[0],
            conv_ssm_b[0][None], jnp.pad(dt_bias[0], (0, hpad))[None], jnp.pad(a_log[0], (0, hpad))[None],
            jnp.repeat(d_skip[0], SSM_HEAD_DIM)[None], norm_ssm[0][None], w_out[0].astype(BF16),
            norm_ffn[0][None], wr3, br)


def _ssm_to_internal(s):
    b = s.shape[0]
    s = s.reshape(b, SSM_GROUPS, SSM_HEADS // SSM_GROUPS, SSM_HEAD_DIM, SSM_STATE)
    return s.transpose(0, 1, 4, 2, 3).reshape(b, SSM_GROUPS, SSM_STATE, D_SSM // SSM_GROUPS)


def _ssm_from_internal(s):
    b = s.shape[0]
    s = s.reshape(b, SSM_GROUPS, SSM_STATE, SSM_HEADS // SSM_GROUPS, SSM_HEAD_DIM)
    return s.transpose(0, 1, 3, 4, 2).reshape(b, SSM_HEADS, SSM_HEAD_DIM, SSM_STATE)


MIX_TL = 512
MIX_Q = 256
COMB_TT = 256


def _routing(top_i):
    t = top_i.shape[0]
    n = t * TOP_K
    assert n % MOE_BLOCK == 0
    nblk = n // MOE_BLOCK
    n_items = nblk + N_EXPERTS
    shift = max(n - 1, 1).bit_length()
    assert N_EXPERTS << shift < 2 ** 31
    flat_e = top_i.reshape(n)
    ar = jnp.arange(n, dtype=jnp.int32)
    key = lax.sort(flat_e * (1 << shift) + ar)
    sorted_e = key >> shift
    slot_tok = (key & ((1 << shift) - 1)) // TOP_K
    experts = jnp.arange(N_EXPERTS, dtype=jnp.int32)
    group_start = jnp.sum((sorted_e[:, None] < experts[None, :]).astype(jnp.int32), axis=0)
    onehot = (flat_e[:, None] == experts[None, :]).astype(jnp.int32)
    rank = jnp.take_along_axis(jnp.cumsum(onehot, axis=0), flat_e[:, None], axis=1)[:, 0] - 1
    pos = group_start[flat_e] + rank
    bp = lax.sort(jnp.concatenate([jnp.arange(nblk, dtype=jnp.int32) * MOE_BLOCK, group_start]))
    dup = jnp.concatenate([jnp.zeros((1,), bool), bp[1:] == bp[:-1]])
    bp = lax.sort(jnp.where(jnp.logical_or(dup, bp >= n), n, bp))
    blk = jnp.minimum(bp // MOE_BLOCK, nblk - 1)
    lo = bp - blk * MOE_BLOCK
    exp = sorted_e[jnp.minimum(bp, n - 1)]
    items = tuple(a.astype(jnp.int32) for a in (blk, exp, lo))
    assert items[0].shape == (n_items,)
    return slot_tok.astype(jnp.int32).reshape(nblk, 1, MOE_BLOCK), pos.astype(jnp.int32), items


def _tile_gather_start(src_hbm, idx_ref, idx_off, n_rows, dst, sem):
    def body(i, carry):
        src_row = pl.multiple_of(idx_ref[0, idx_off + i] * ROW_TILES, ROW_TILES)
        dst_row = pl.multiple_of(i * ROW_TILES, ROW_TILES)
        pltpu.make_async_copy(src_hbm.at[pl.ds(src_row, ROW_TILES), :],
                              dst.at[pl.ds(dst_row, ROW_TILES), :], sem).start()
        return carry
    lax.fori_loop(0, n_rows, body, 0, unroll=8)


def _tile_gather_start_inline(src_hbm, idx_ref, idx_off, n_rows, dst, sem):
    for i in range(n_rows):
        src_row = pl.multiple_of(idx_ref[0, idx_off + i] * ROW_TILES, ROW_TILES)
        pltpu.make_async_copy(src_hbm.at[pl.ds(src_row, ROW_TILES), :],
                              dst.at[pl.ds(i * ROW_TILES, ROW_TILES), :], sem).start()


RING = 3


def _expert_kernel(blk_ref, exp_ref, lo_ref, tok_cur, tok_nxt, tok_far, xf_hbm,
                   wg_ref, bg_ref, wu_ref, bu_ref, wd_ref, bd_ref, yb_ref, xbuf, sem, wgb, wub, wdb, *, n_items):
    w = pl.program_id(0)
    lo = lo_ref[w]
    slot = lax.rem(w, RING)
    far = lax.rem(w + RING - 1, RING)

    def wait_rows(s):
        pltpu.make_async_copy(xf_hbm.at[pl.ds(0, MOE_BLOCK * ROW_TILES), :], xbuf.at[s], sem.at[s]).wait()

    @pl.when(w == 0)
    def _():
        _tile_gather_start(xf_hbm, tok_cur, 0, MOE_BLOCK, xbuf.at[0], sem.at[0])
        _tile_gather_start(xf_hbm, tok_nxt, 0, MOE_BLOCK, xbuf.at[1], sem.at[1])

    @pl.when(jnp.logical_or(w == 0, exp_ref[w] != exp_ref[jnp.maximum(w - 1, 0)]))
    def _():
        wgb[...] = wg_ref[...].astype(BF16)
        wub[...] = wu_ref[...].astype(BF16)
        wdb[...] = wd_ref[...].astype(BF16)

    wait_rows(slot)
    xb = _load_token_tiles(xbuf.at[slot], MOE_BLOCK).astype(BF16)
    gate = jnp.dot(xb, wgb[...], preferred_element_type=F32) + bg_ref[...]
    up = jnp.dot(xb, wub[...], preferred_element_type=F32) + bu_ref[...]
    gate = jnp.minimum(gate, SWIGLU_LIMIT)
    up = jnp.clip(up, -SWIGLU_LIMIT, SWIGLU_LIMIT)
    glu = gate * jax.nn.sigmoid(gate * SWIGLU_ALPHA)
    h = ((up + 1.0) * glu).astype(BF16)
    y = jnp.dot(h, wdb[...], preferred_element_type=F32) + bd_ref[...]
    keep = lax.broadcasted_iota(jnp.int32, (MOE_BLOCK, D_MODEL), 0) < lo
    _store_token_tiles(yb_ref, jnp.where(keep, _load_token_tiles(yb_ref, MOE_BLOCK), y))
    _tile_gather_start_inline(xf_hbm, tok_far, 0, MOE_BLOCK, xbuf.at[far], sem.at[far])

    @pl.when(w == n_items - 1)
    def _():
        for s in range(1, RING):
            wait_rows(lax.rem(w + s, RING))


def _expert_call(slot_tok, items, xf, w_gate, b_gate, w_up, b_up, w_down, b_down):
    nblk = slot_tok.shape[0]
    n_items = items[0].shape[0]
    d, de = w_gate.shape[1], w_gate.shape[2]
    tok_spec = lambda f: pl.BlockSpec((None, 1, MOE_BLOCK), f, memory_space=pltpu.SMEM)
    w_spec = lambda r, c: pl.BlockSpec((None, r, c), lambda w, blk, exp, lo: (exp[w], 0, 0))
    grid_spec = pltpu.PrefetchScalarGridSpec(
        num_scalar_prefetch=3,
        grid=(n_items,),
        in_specs=[
            tok_spec(lambda w, blk, exp, lo: (blk[w], 0, 0)),
            tok_spec(lambda w, blk, exp, lo: (blk[jnp.minimum(w + 1, n_items - 1)], 0, 0)),
            tok_spec(lambda w, blk, exp, lo: (blk[jnp.minimum(w + RING - 1, n_items - 1)], 0, 0)),
            pl.BlockSpec(memory_space=pl.ANY),
            w_spec(d, de), w_spec(1, de), w_spec(d, de), w_spec(1, de), w_spec(de, d), w_spec(1, d),
        ],
        out_specs=pl.BlockSpec((MOE_BLOCK * ROW_TILES, LANES), lambda w, blk, exp, lo: (blk[w], 0)),
        scratch_shapes=[
            pltpu.VMEM((RING, MOE_BLOCK * ROW_TILES, LANES), F32),
            pltpu.SemaphoreType.DMA((RING,)),
            pltpu.VMEM((d, de), BF16), pltpu.VMEM((d, de), BF16), pltpu.VMEM((de, d), BF16),
        ],
    )
    return pl.pallas_call(
        functools.partial(_expert_kernel, n_items=n_items),
        grid_spec=grid_spec,
        out_shape=jax.ShapeDtypeStruct((nblk * MOE_BLOCK * ROW_TILES, LANES), F32),
        compiler_params=pltpu.CompilerParams(dimension_semantics=("arbitrary",), vmem_limit_bytes=VMEM_LIMIT),
        name="experts",
    )(*items, slot_tok, slot_tok, slot_tok, xf, w_gate, b_gate[:, None, :], w_up, b_up[:, None, :],
      w_down, b_down[:, None, :])


def _combine_kernel(pos_cur, pos_nxt, pos_far, x1_ref, rrow_ref, yb_hbm, fn_ref, out_ref, ybuf, sem, *, n):
    i = pl.program_id(0)
    slot = lax.rem(i, RING)
    far = lax.rem(i + RING - 1, RING)
    tt = x1_ref.shape[0]

    def wait_rows(s):
        for k in range(TOP_K):
            pltpu.make_async_copy(yb_hbm.at[pl.ds(0, tt * ROW_TILES), :], ybuf.at[s, k], sem.at[s]).wait()

    @pl.when(i == 0)
    def _():
        for k in range(TOP_K):
            _tile_gather_start(yb_hbm, pos_cur, k * tt, tt, ybuf.at[0, k], sem.at[0])
            _tile_gather_start(yb_hbm, pos_nxt, k * tt, tt, ybuf.at[1, k], sem.at[1])

    wait_rows(slot)
    rr = rrow_ref[...]
    acc = x1_ref[...]
    for k in range(TOP_K):
        acc = acc + rr[:, TOP_K + k:TOP_K + k + 1] * _load_token_tiles(ybuf.at[slot, k], tt)
    out_ref[...] = acc * lax.rsqrt(jnp.mean(acc * acc, axis=-1, keepdims=True) + EPS) * fn_ref[...]
    for k in range(TOP_K):
        _tile_gather_start_inline(yb_hbm, pos_far, k * tt, tt, ybuf.at[far, k], sem.at[far])

    @pl.when(i == n - 1)
    def _():
        for s in range(1, RING):
            wait_rows(lax.rem(i + s, RING))


def _combine_call(pos, x1, rrow, yb, fn, *, row_off, n_rows):
    tt = min(COMB_TT, n_rows)
    n_tiles = n_rows // tt
    off = row_off // tt
    d = x1.shape[1]
    pos = pos[row_off * TOP_K:(row_off + n_rows) * TOP_K].reshape(n_tiles, tt, TOP_K)
    pos = pos.transpose(0, 2, 1).reshape(n_tiles, 1, TOP_K * tt)
    pos_spec = lambda f: pl.BlockSpec((None, 1, tt * TOP_K), f, memory_space=pltpu.SMEM)
    return pl.pallas_call(
        functools.partial(_combine_kernel, n=n_tiles),
        grid=(n_tiles,),
        in_specs=[
            pos_spec(lambda i: (i, 0, 0)),
            pos_spec(lambda i: (jnp.minimum(i + 1, n_tiles - 1), 0, 0)),
            pos_spec(lambda i: (jnp.minimum(i + RING - 1, n_tiles - 1), 0, 0)),
            pl.BlockSpec((tt, d), lambda i: (off + i, 0)),
            pl.BlockSpec((tt, LANES), lambda i: (off + i, 0)),
            pl.BlockSpec(memory_space=pl.ANY),
            pl.BlockSpec((1, d), lambda i: (0, 0)),
        ],
        out_specs=pl.BlockSpec((tt, d), lambda i: (i, 0)),
        out_shape=jax.ShapeDtypeStruct((n_rows, d), F32),
        scratch_shapes=[pltpu.VMEM((RING, TOP_K, tt * ROW_TILES, LANES), F32), pltpu.SemaphoreType.DMA((RING,))],
        compiler_params=pltpu.CompilerParams(dimension_semantics=("arbitrary",), vmem_limit_bytes=VMEM_LIMIT),
        name="combine",
    )(pos, pos, pos, x1, rrow, yb, fn)


def kernel(x_prompt, x_sample, cache_conv_a, cache_conv_ssm, state_ssm, norm_mix, w_in, conv_a_w, norm_conv_a, conv_ssm_w, conv_ssm_b, dt_bias, a_log, d_skip, norm_ssm, w_out, norm_ffn, w_router, b_router, w_gate, b_gate, w_up, b_up, w_down, b_down, final_norm):
    bp, lp, _ = x_prompt.shape
    bs, ls, _ = x_sample.shape
    t_p, t_s = bp * lp, bs * ls
    t_all = t_p + t_s
    wts = _prep_mixer_weights(norm_mix, w_in, conv_a_w, norm_conv_a, conv_ssm_w, conv_ssm_b, dt_bias, a_log,
                              d_skip, norm_ssm, w_out, norm_ffn, w_router, b_router)
    zeros = lambda *s: jnp.zeros(s, F32)
    x1, xf, rrow, route_p, ca_p, cs_p, st_p = _mixer_call(
        x_prompt, zeros(bp, CONV_A_WIDTH - 1, D_CONV), zeros(bp, SSM_CONV_WIDTH - 1, D_XBC),
        zeros(bp, SSM_GROUPS, SSM_STATE, D_SSM // SSM_GROUPS), wts, _mixer_consts(MIX_Q),
        tl=MIX_TL, q=MIX_Q, t_all=t_all, row_off=0)
    x1, xf, rrow, route_s, ca_s, cs_s, st_s = _mixer_call(
        x_sample, cache_conv_a[0], cache_conv_ssm[0], _ssm_to_internal(state_ssm[0]), wts, _mixer_consts(ls),
        tl=ls, q=ls, t_all=t_all, row_off=t_p, alias=(x1, xf, rrow))

    top_i = jnp.concatenate([route_p[:, 0:TOP_K, :].transpose(0, 2, 1).reshape(t_p, TOP_K),
                             route_s[:, 0:TOP_K, :].transpose(0, 2, 1).reshape(t_s, TOP_K)], axis=0)
    slot_tok, pos, items = _routing(top_i.astype(jnp.int32))

    yb = _expert_call(slot_tok, items, xf, w_gate[0], b_gate[0], w_up[0], b_up[0], w_down[0], b_down[0])
    fn = final_norm[None]
    y_p = _combine_call(pos, x1, rrow, yb, fn, row_off=0, n_rows=t_p)
    y_s = _combine_call(pos, x1, rrow, yb, fn, row_off=t_p, n_rows=t_s)
    return (y_p.reshape(bp, lp, D_MODEL), y_s.reshape(bs, ls, D_MODEL),
            ca_p[None], cs_p[None], _ssm_from_internal(st_p)[None],
            ca_s[None], cs_s[None], _ssm_from_internal(st_s)[None])
```

```python
import functools

import jax
import jax.numpy as jnp
from jax import lax
from jax.experimental import pallas as pl
from jax.experimental.pallas import tpu as pltpu

F32 = jnp.float32
BF16 = jnp.bfloat16

D_MODEL = 1024
D_CONV = 1024
D_SSM = 1024
CONV_GROUPS = 16
CONV_A_WIDTH = 3
SSM_HEAD_DIM = 64
SSM_HEADS = 16
SSM_GROUPS = 2
SSM_STATE = 128
SSM_CONV_WIDTH = 4
D_XBC = D_SSM + 2 * SSM_GROUPS * SSM_STATE
N_EXPERTS = 32
TOP_K = 4
SWIGLU_LIMIT = 7.0
SWIGLU_ALPHA = 1.702
MOE_BLOCK = 256
EPS = 1e-5

LANES = 128
COL_A = 3 * D_CONV
COL_S = D_SSM + D_XBC + LANES
D_IN_PAD = COL_A + COL_S
assert 2 * SSM_HEAD_DIM == LANES and SSM_STATE == LANES
NEG_BIG = -1e30
VMEM_LIMIT = 56 * 1024 * 1024


def _softplus(v):
    return jnp.maximum(v, 0.0) + jnp.log1p(jnp.exp(-jnp.abs(v)))


def _silu(v):
    return v * jax.nn.sigmoid(v)


def _split2(v):
    hi = v.astype(BF16)
    mid = (v - hi.astype(F32)).astype(BF16)
    return hi, mid


ROUTE_ROWS = 16
ROW_TILES = D_MODEL // LANES


def _store_token_tiles(ref, v, row0=0):
    r = v.shape[0]
    for s in range(ROW_TILES):
        ref[pl.ds(row0 * ROW_TILES + s, r, stride=ROW_TILES), :] = v[:, s * LANES:(s + 1) * LANES]


def _load_token_tiles(ref, r):
    return jnp.concatenate([ref[pl.ds(s, r, stride=ROW_TILES), :] for s in range(ROW_TILES)], axis=1)


def _transpose_rows(v):
    r = v.shape[0]
    if r % LANES:
        v = jnp.concatenate([v, jnp.zeros((LANES - r % LANES, v.shape[1]), v.dtype)], axis=0)
    return v.T[:, :r]


def _ssd_chunk(xs, bm, cm, dt, a_neg, z, dskip, s_scr, tril, e2):
    q = xs.shape[0]
    da = dt * a_neg
    d_hi = da.astype(BF16)
    r1 = da - d_hi.astype(F32)
    d_mid = r1.astype(BF16)
    d_lo = (r1 - d_mid.astype(F32)).astype(BF16)
    cs = jnp.dot(tril, jnp.concatenate([d_hi, d_mid, d_lo], axis=1), preferred_element_type=F32)
    acum = cs[:, 0:LANES] + cs[:, LANES:2 * LANES] + cs[:, 2 * LANES:3 * LANES]
    acum_t = _transpose_rows(acum)
    total = acum[q - 1:q, :]
    stack = jnp.concatenate(
        [dt, jnp.exp(total - acum), jnp.exp(acum), jnp.broadcast_to(jnp.exp(total), (8, LANES))], axis=0)
    s_hi, s_mid = _split2(stack)
    ex = jnp.dot(jnp.concatenate([s_hi, s_mid], axis=1), e2, preferred_element_type=F32)
    dt_e, dec_e, ea_e, cd_e = ex[0:q], ex[q:2 * q], ex[2 * q:3 * q], ex[3 * q:3 * q + 1]
    xdt = xs * dt_e
    xdt_b = xdt.astype(BF16)
    xdec_b = (xdt * dec_e).astype(BF16)
    row = lax.broadcasted_iota(jnp.int32, (q, q), 0)
    col = lax.broadcasted_iota(jnp.int32, (q, q), 1)
    causal = row >= col
    lane = lax.broadcasted_iota(jnp.int32, (q, LANES), 1)
    gw = D_SSM // SSM_GROUPS
    hpg = SSM_HEADS // SSM_GROUPS
    ys = []
    for g in range(SSM_GROUPS):
        cg = cm[:, g * SSM_STATE:(g + 1) * SSM_STATE].astype(BF16)
        bg = bm[:, g * SSM_STATE:(g + 1) * SSM_STATE].astype(BF16)
        cb = lax.dot_general(cg, bg, (((1,), (1,)), ((), ())), preferred_element_type=F32)
        s_g = s_scr[g]
        y_off = jnp.dot(cg, s_g.astype(BF16), preferred_element_type=F32) * ea_e[:, g * gw:(g + 1) * gw]
        pieces = []
        for j in range(hpg // 2):
            h0 = g * hpg + 2 * j
            ms = []
            for h in (h0, h0 + 1):
                diff = acum[:, h:h + 1] - acum_t[h:h + 1, :]
                ms.append(cb * jnp.exp(jnp.where(causal, diff, NEG_BIG)))
            m = jnp.concatenate(ms, axis=1).astype(BF16)
            xp = xdt_b[:, h0 * SSM_HEAD_DIM:(h0 + 2) * SSM_HEAD_DIM]
            zero = jnp.zeros_like(xp)
            rhs = jnp.concatenate([jnp.where(lane < SSM_HEAD_DIM, xp, zero),
                                   jnp.where(lane >= SSM_HEAD_DIM, xp, zero)], axis=0)
            pieces.append(jnp.dot(m, rhs, preferred_element_type=F32))
        ys.append(jnp.concatenate(pieces, axis=1) + y_off)
        upd = lax.dot_general(bg, xdec_b[:, g * gw:(g + 1) * gw], (((0,), (0,)), ((), ())),
                              preferred_element_type=F32)
        s_scr[g] = s_g * cd_e[:, g * gw:(g + 1) * gw] + upd
    y = jnp.concatenate(ys, axis=1) + dskip * xs
    return y * _silu(z)


def _mixer_kernel(*refs, tl, q, nl, n_alias):
    (x_ref, ca0_ref, cs0_ref, ssm0_ref, cnt0_ref, nmix_ref, win_ref, caw_ref, na_ref, csw_ref, csb_ref,
     dtb_ref, alog_ref, dsk_ref, ns_ref, wout_ref, nffn_ref, wr_ref, br_ref,
     tril_ref, e2_ref, g_ref, e3_ref) = refs[:23]
    refs = refs[23 + n_alias:]
    (x1_ref, xf_ref, rrow_ref, route_ref, ca_ref, cs_ref, ssm_ref, cnt_ref,
     upad, xpad, s_scr, ca_scr, cs_scr, cnt_scr) = refs
    l = pl.program_id(1)
    last = nl - 1
    pa_rows = CONV_A_WIDTH - 1
    ps_rows = SSM_CONV_WIDTH - 1

    @pl.when(jnp.logical_and(pl.program_id(0) == 0, l == 0))
    def _():
        cnt_scr[...] = cnt0_ref[...]

    @pl.when(l == 0)
    def _():
        ca_scr[...] = ca0_ref[...]
        cs_scr[...] = cs0_ref[...]
        for hp in range(SSM_HEADS // 2):
            g, j = divmod(hp, SSM_HEADS // SSM_GROUPS // 2)
            pair = jnp.concatenate([ssm0_ref[2 * hp], ssm0_ref[2 * hp + 1]], axis=0)
            s_scr[g, :, j * LANES:(j + 1) * LANES] = pair.T

    caw = caw_ref[...]
    csw = csw_ref[...]
    a_neg = -jnp.exp(alog_ref[...])
    gam = jnp.concatenate([na_ref[...], ns_ref[...]], axis=1)
    nb = D_SSM + SSM_GROUPS * SSM_STATE
    lane = lax.broadcasted_iota(jnp.int32, (q, LANES), 1)
    lane_f = lane.astype(F32)
    tail_a = ca_scr[...]
    tail_s = cs_scr[...]
    for j in range(tl // q):
        rows = slice(j * q, (j + 1) * q)
        x = x_ref[rows, :]
        xn = (x * lax.rsqrt(jnp.mean(x * x, axis=-1, keepdims=True) + EPS) * nmix_ref[...]).astype(BF16)

        pa = jnp.dot(xn, win_ref[:, 0:COL_A], preferred_element_type=F32)
        a_b, a_c, a_h = pa[:, 0:D_CONV], pa[:, D_CONV:2 * D_CONV], pa[:, 2 * D_CONV:3 * D_CONV]
        u = a_c * a_h
        upad[j, 8 - pa_rows:8, :] = tail_a
        upad[j, 8:8 + q, :] = u
        conv = upad[j, 8 - pa_rows:8 - pa_rows + q, :] * caw[0:1, :]
        for k in range(1, CONV_A_WIDTH):
            conv = conv + upad[j, 8 - pa_rows + k:8 - pa_rows + k + q, :] * caw[k:k + 1, :]
        y_a = a_b * conv
        tail_a = u[q - pa_rows:q, :]

        ps = jnp.dot(xn, win_ref[:, COL_A:D_IN_PAD], preferred_element_type=F32)
        z = ps[:, 0:D_SSM]
        xbc = ps[:, D_SSM:D_SSM + D_XBC]
        xpad[j, 8 - ps_rows:8, :] = tail_s
        xpad[j, 8:8 + q, :] = xbc
        dt = _softplus(ps[:, D_SSM + D_XBC:COL_S] + dtb_ref[...])
        acc = xpad[j, 8 - ps_rows:8 - ps_rows + q, :] * csw[0:1, :]
        for k in range(1, SSM_CONV_WIDTH):
            acc = acc + xpad[j, 8 - ps_rows + k:8 - ps_rows + k + q, :] * csw[k:k + 1, :]
        xc = _silu(acc + csb_ref[...])
        tail_s = xbc[q - ps_rows:q, :]
        y_s = _ssd_chunk(xc[:, 0:D_SSM], xc[:, D_SSM:nb], xc[:, nb:D_XBC], dt, a_neg, z, dsk_ref[...],
                         s_scr, tril_ref[...], e2_ref[...])

        mixed = jnp.concatenate([y_a, y_s], axis=1)
        mean = jnp.dot((mixed * mixed).astype(BF16), g_ref[...], preferred_element_type=F32)
        r_hi, r_mid = _split2(lax.rsqrt(mean + EPS))
        rs_e = jnp.dot(jnp.concatenate([r_hi, r_mid], axis=1), e3_ref[...], preferred_element_type=F32)
        mixed_n = (mixed * rs_e * gam).astype(BF16)
        x1 = x + jnp.dot(mixed_n, wout_ref[...], preferred_element_type=F32)
        x1_ref[rows, :] = x1

        xf = x1 * lax.rsqrt(jnp.mean(x1 * x1, axis=-1, keepdims=True) + EPS) * nffn_ref[...]
        _store_token_tiles(xf_ref, xf, row0=j * q)
        f_hi, f_mid = _split2(xf)
        logits = jnp.dot(jnp.concatenate([f_hi, f_hi, f_mid], axis=1), wr_ref[...],
                         preferred_element_type=F32) + br_ref[...]
        work = logits
        vals, idxs = [], []
        for _ in range(TOP_K):
            m = jnp.max(work, axis=-1, keepdims=True)
            idx = jnp.min(jnp.where(work == m, lane_f, float(LANES)), axis=-1, keepdims=True)
            vals.append(m)
            idxs.append(idx)
            work = jnp.where(lane_f == idx, -jnp.inf, work)
        es = [jnp.exp(v - vals[0]) for v in vals]
        inv = 1.0 / (es[0] + es[1] + es[2] + es[3])
        hots = [(lane_f == idxs[k]).astype(F32) for k in range(TOP_K)]
        hot = hots[0] + hots[1] + hots[2] + hots[3]
        before = jnp.dot(tril_ref[...], hot.astype(BF16), preferred_element_type=F32) - hot + cnt_scr[...]
        cnt_scr[...] = before[q - 1:q, :] + hot[q - 1:q, :]
        route = jnp.zeros((q, LANES), F32)
        for k in range(TOP_K):
            route = jnp.where(lane == k, idxs[k], route)
            route = jnp.where(lane == TOP_K + k, es[k] * inv, route)
            route = jnp.where(lane == 2 * TOP_K + k, jnp.sum(hots[k] * before, axis=-1, keepdims=True), route)
        rrow_ref[rows, :] = route
        route_ref[:, rows] = _transpose_rows(route)[0:ROUTE_ROWS, :]

    ca_scr[...] = tail_a
    cs_scr[...] = tail_s
    cnt_ref[...] = cnt_scr[...]

    @pl.when(l == last)
    def _():
        ca_ref[...] = tail_a
        cs_ref[...] = tail_s
        for hp in range(SSM_HEADS // 2):
            g, j = divmod(hp, SSM_HEADS // SSM_GROUPS // 2)
            pair = s_scr[g, :, j * LANES:(j + 1) * LANES].T
            ssm_ref[2 * hp] = pair[0:SSM_HEAD_DIM, :]
            ssm_ref[2 * hp + 1] = pair[SSM_HEAD_DIM:2 * SSM_HEAD_DIM, :]


def _const_spec(shape):
    nd = len(shape)
    return pl.BlockSpec(shape, lambda b, l: (0,) * nd, pipeline_mode=pl.Buffered(1))


def _mixer_call(x, ca0, cs0, ssm0, cnt0, wts, consts, *, tl, q, t_all, row_off, alias=None):
    b_sz, l_sz, _ = x.shape
    nl = l_sz // tl
    blk_off = row_off // tl
    n_alias = 0 if alias is None else len(alias)
    in_specs = [
        pl.BlockSpec((None, tl, D_MODEL), lambda b, l: (b, l, 0)),
        pl.BlockSpec((None, CONV_A_WIDTH - 1, D_CONV), lambda b, l: (b, 0, 0)),
        pl.BlockSpec((None, SSM_CONV_WIDTH - 1, D_XBC), lambda b, l: (b, 0, 0)),
        pl.BlockSpec((None, SSM_HEADS, SSM_HEAD_DIM, SSM_STATE), lambda b, l: (b, 0, 0, 0)),
        _const_spec(cnt0.shape),
    ] + [_const_spec(w.shape) for w in wts] + [_const_spec(c.shape) for c in consts]
    in_specs += [pl.BlockSpec(memory_space=pl.ANY)] * n_alias
    row_spec = pl.BlockSpec((tl, D_MODEL), lambda b, l: (blk_off + b * nl + l, 0))
    out_specs = [
        row_spec,
        pl.BlockSpec((tl * ROW_TILES, LANES), lambda b, l: (blk_off + b * nl + l, 0)),
        pl.BlockSpec((tl, LANES), lambda b, l: (blk_off + b * nl + l, 0)),
        pl.BlockSpec((None, ROUTE_ROWS, tl), lambda b, l: (b, 0, l)),
        pl.BlockSpec((None, CONV_A_WIDTH - 1, D_CONV), lambda b, l: (b, 0, 0)),
        pl.BlockSpec((None, SSM_CONV_WIDTH - 1, D_XBC), lambda b, l: (b, 0, 0)),
        pl.BlockSpec((None, SSM_HEADS, SSM_HEAD_DIM, SSM_STATE), lambda b, l: (b, 0, 0, 0)),
        pl.BlockSpec((1, LANES), lambda b, l: (0, 0)),
    ]
    out_shape = [
        jax.ShapeDtypeStruct((t_all, D_MODEL), F32),
        jax.ShapeDtypeStruct((t_all * ROW_TILES, LANES), F32),
        jax.ShapeDtypeStruct((t_all, LANES), F32),
        jax.ShapeDtypeStruct((b_sz, ROUTE_ROWS, l_sz), F32),
        jax.ShapeDtypeStruct((b_sz, CONV_A_WIDTH - 1, D_CONV), F32),
        jax.ShapeDtypeStruct((b_sz, SSM_CONV_WIDTH - 1, D_XBC), F32),
        jax.ShapeDtypeStruct((b_sz, SSM_HEADS, SSM_HEAD_DIM, SSM_STATE), F32),
        jax.ShapeDtypeStruct((1, LANES), F32),
    ]
    n_in = 5 + len(wts) + len(consts)
    aliases = {n_in + i: i for i in range(n_alias)}
    args = [x, ca0, cs0, ssm0, cnt0, *wts, *consts] + ([] if alias is None else list(alias))
    return pl.pallas_call(
        functools.partial(_mixer_kernel, tl=tl, q=q, nl=nl, n_alias=n_alias),
        grid=(b_sz, nl),
        in_specs=in_specs,
        out_specs=out_specs,
        out_shape=out_shape,
        scratch_shapes=[
            pltpu.VMEM((tl // q, q + 8, D_CONV), F32),
            pltpu.VMEM((tl // q, q + 8, D_XBC), F32),
            pltpu.VMEM((SSM_GROUPS, SSM_STATE, D_SSM // SSM_GROUPS), F32),
            pltpu.VMEM((CONV_A_WIDTH - 1, D_CONV), F32),
            pltpu.VMEM((SSM_CONV_WIDTH - 1, D_XBC), F32),
            pltpu.VMEM((1, LANES), F32),
        ],
        input_output_aliases=aliases,
        compiler_params=pltpu.CompilerParams(
            dimension_semantics=("arbitrary", "arbitrary"), vmem_limit_bytes=VMEM_LIMIT),
        name="mixer",
    )(*args)


def _mixer_consts(q):
    r = jnp.arange(q)
    tril = (r[:, None] >= r[None, :]).astype(BF16)
    j2 = jnp.arange(2 * LANES) % LANES
    e2 = (j2[:, None] == (jnp.arange(D_SSM) // SSM_HEAD_DIM)[None, :]).astype(BF16)
    c = jnp.arange(D_CONV + D_SSM)
    grp = jnp.where(c < D_CONV, c // (D_CONV // CONV_GROUPS),
                    CONV_GROUPS + (c - D_CONV) // (D_SSM // SSM_GROUPS))
    wgt = jnp.where(c < D_CONV, CONV_GROUPS / D_CONV, SSM_GROUPS / D_SSM)
    g = jnp.where(grp[:, None] == jnp.arange(LANES)[None, :], wgt[:, None], 0.0).astype(BF16)
    e3 = (j2[:, None] == grp[None, :]).astype(BF16)
    return tril, e2, g, e3


def _prep_mixer_weights(norm_mix, w_in, conv_a_w, norm_conv_a, conv_ssm_w, conv_ssm_b, dt_bias, a_log,
                        d_skip, norm_ssm, w_out, norm_ffn, w_router, b_router):
    hpad = LANES - SSM_HEADS
    w_in_p = jnp.pad(w_in[0], ((0, 0), (0, hpad))).astype(BF16)
    wr = jnp.pad(w_router[0], ((0, 0), (0, LANES - N_EXPERTS)))
    wr_hi = wr.astype(BF16)
    wr_mid = (wr - wr_hi.astype(F32)).astype(BF16)
    wr3 = jnp.concatenate([wr_hi, wr_mid, wr_hi], axis=0)
    br = jnp.pad(b_router[0], (0, LANES - N_EXPERTS), constant_values=NEG_BIG)[None]
    return (norm_mix[0][None], w_in_p, conv_a_w[0], norm_conv_a[0][None], conv_ssm_w[0],
            conv_ssm_b[0][None], jnp.pad(dt_bias[0], (0, hpad))[None], jnp.pad(a_log[0], (0, hpad))[None],
            jnp.repeat(d_skip[0], SSM_HEAD_DIM)[None], norm_ssm[0][None], w_out[0].astype(BF16),
            norm_ffn[0][None], wr3, br)


MIX_TL = 512
MIX_Q = 256
COMB_TT = 256


def _routing(top_i, rank, counts):
    t = top_i.shape[0]
    n = t * TOP_K
    assert n % MOE_BLOCK == 0
    nblk = n // MOE_BLOCK
    n_items = nblk + N_EXPERTS
    shift = max(n - 1, 1).bit_length()
    assert N_EXPERTS << shift < 2 ** 31
    flat_e = top_i.reshape(n)
    ar = jnp.arange(n, dtype=jnp.int32)
    key = lax.sort(flat_e * (1 << shift) + ar)
    slot_tok = (key & ((1 << shift) - 1)) // TOP_K
    group_end = jnp.cumsum(counts)
    group_start = group_end - counts
    pos = group_start[flat_e] + rank.reshape(n)
    bp = lax.sort(jnp.concatenate([jnp.arange(nblk, dtype=jnp.int32) * MOE_BLOCK, group_start]))
    dup = jnp.concatenate([jnp.zeros((1,), bool), bp[1:] == bp[:-1]])
    bp = lax.sort(jnp.where(jnp.logical_or(dup, bp >= n), n, bp))
    blk = jnp.minimum(bp // MOE_BLOCK, nblk - 1)
    lo = bp - blk * MOE_BLOCK
    exp = jnp.sum((group_end[None, :] <= jnp.minimum(bp, n - 1)[:, None]).astype(jnp.int32), axis=1)
    items = tuple(a.astype(jnp.int32) for a in (blk, exp, lo))
    assert items[0].shape == (n_items,)
    return slot_tok.astype(jnp.int32).reshape(nblk, 1, MOE_BLOCK), pos.astype(jnp.int32), items


def _tile_gather_start(src_hbm, idx_ref, idx_off, n_rows, dst, sem):
    def body(i, carry):
        src_row = pl.multiple_of(idx_ref[0, idx_off + i] * ROW_TILES, ROW_TILES)
        dst_row = pl.multiple_of(i * ROW_TILES, ROW_TILES)
        pltpu.make_async_copy(src_hbm.at[pl.ds(src_row, ROW_TILES), :],
                              dst.at[pl.ds(dst_row, ROW_TILES), :], sem).start()
        return carry
    lax.fori_loop(0, n_rows, body, 0, unroll=8)


def _tile_gather_start_inline(src_hbm, idx_ref, idx_off, n_rows, dst, sem):
    for i in range(n_rows):
        src_row = pl.multiple_of(idx_ref[0, idx_off + i] * ROW_TILES, ROW_TILES)
        pltpu.make_async_copy(src_hbm.at[pl.ds(src_row, ROW_TILES), :],
                              dst.at[pl.ds(i * ROW_TILES, ROW_TILES), :], sem).start()


RING = 3


def _expert_kernel(blk_ref, exp_ref, lo_ref, tok_cur, tok_nxt, tok_far, xf_hbm,
                   wg_ref, bg_ref, wu_ref, bu_ref, wd_ref, bd_ref, yb_ref, xbuf, sem, wgb, wub, wdb, *, n_items):
    w = pl.program_id(0)
    lo = lo_ref[w]
    slot = lax.rem(w, RING)
    far = lax.rem(w + RING - 1, RING)

    def wait_rows(s):
        pltpu.make_async_copy(xf_hbm.at[pl.ds(0, MOE_BLOCK * ROW_TILES), :], xbuf.at[s], sem.at[s]).wait()

    @pl.when(w == 0)
    def _():
        _tile_gather_start(xf_hbm, tok_cur, 0, MOE_BLOCK, xbuf.at[0], sem.at[0])
        _tile_gather_start(xf_hbm, tok_nxt, 0, MOE_BLOCK, xbuf.at[1], sem.at[1])

    @pl.when(jnp.logical_or(w == 0, exp_ref[w] != exp_ref[jnp.maximum(w - 1, 0)]))
    def _():
        wgb[...] = wg_ref[...].astype(BF16)
        wub[...] = wu_ref[...].astype(BF16)
        wdb[...] = wd_ref[...].astype(BF16)

    wait_rows(slot)
    xb = _load_token_tiles(xbuf.at[slot], MOE_BLOCK).astype(BF16)
    gate = jnp.dot(xb, wgb[...], preferred_element_type=F32) + bg_ref[...]
    up = jnp.dot(xb, wub[...], preferred_element_type=F32) + bu_ref[...]
    gate = jnp.minimum(gate, SWIGLU_LIMIT)
    up = jnp.clip(up, -SWIGLU_LIMIT, SWIGLU_LIMIT)
    glu = gate * jax.nn.sigmoid(gate * SWIGLU_ALPHA)
    h = ((up + 1.0) * glu).astype(BF16)
    y = jnp.dot(h, wdb[...], preferred_element_type=F32) + bd_ref[...]
    keep = lax.broadcasted_iota(jnp.int32, (MOE_BLOCK, D_MODEL), 0) < lo
    _store_token_tiles(yb_ref, jnp.where(keep, _load_token_tiles(yb_ref, MOE_BLOCK), y))
    _tile_gather_start_inline(xf_hbm, tok_far, 0, MOE_BLOCK, xbuf.at[far], sem.at[far])

    @pl.when(w == n_items - 1)
    def _():
        for s in range(1, RING):
            wait_rows(lax.rem(w + s, RING))


def _expert_call(slot_tok, items, xf, w_gate, b_gate, w_up, b_up, w_down, b_down):
    nblk = slot_tok.shape[0]
    n_items = items[0].shape[0]
    d, de = w_gate.shape[1], w_gate.shape[2]
    tok_spec = lambda f: pl.BlockSpec((None, 1, MOE_BLOCK), f, memory_space=pltpu.SMEM)
    w_spec = lambda r, c: pl.BlockSpec((None, r, c), lambda w, blk, exp, lo: (exp[w], 0, 0))
    grid_spec = pltpu.PrefetchScalarGridSpec(
        num_scalar_prefetch=3,
        grid=(n_items,),
        in_specs=[
            tok_spec(lambda w, blk, exp, lo: (blk[w], 0, 0)),
            tok_spec(lambda w, blk, exp, lo: (blk[jnp.minimum(w + 1, n_items - 1)], 0, 0)),
            tok_spec(lambda w, blk, exp, lo: (blk[jnp.minimum(w + RING - 1, n_items - 1)], 0, 0)),
            pl.BlockSpec(memory_space=pl.ANY),
            w_spec(d, de), w_spec(1, de), w_spec(d, de), w_spec(1, de), w_spec(de, d), w_spec(1, d),
        ],
        out_specs=pl.BlockSpec((MOE_BLOCK * ROW_TILES, LANES), lambda w, blk, exp, lo: (blk[w], 0)),
        scratch_shapes=[
            pltpu.VMEM((RING, MOE_BLOCK * ROW_TILES, LANES), F32),
            pltpu.SemaphoreType.DMA((RING,)),
            pltpu.VMEM((d, de), BF16), pltpu.VMEM((d, de), BF16), pltpu.VMEM((de, d), BF16),
        ],
    )
    return pl.pallas_call(
        functools.partial(_expert_kernel, n_items=n_items),
        grid_spec=grid_spec,
        out_shape=jax.ShapeDtypeStruct((nblk * MOE_BLOCK * ROW_TILES, LANES), F32),
        compiler_params=pltpu.CompilerParams(dimension_semantics=("arbitrary",), vmem_limit_bytes=VMEM_LIMIT),
        name="experts",
    )(*items, slot_tok, slot_tok, slot_tok, xf, w_gate, b_gate[:, None, :], w_up, b_up[:, None, :],
      w_down, b_down[:, None, :])


def _combine_kernel(pos_cur, pos_nxt, pos_far, x1_ref, rrow_ref, yb_hbm, fn_ref, out_ref, ybuf, sem, *, n):
    i = pl.program_id(0)
    slot = lax.rem(i, RING)
    far = lax.rem(i + RING - 1, RING)
    tt = x1_ref.shape[0]

    def wait_rows(s):
        for k in range(TOP_K):
            pltpu.make_async_copy(yb_hbm.at[pl.ds(0, tt * ROW_TILES), :], ybuf.at[s, k], sem.at[s]).wait()

    @pl.when(i == 0)
    def _():
        for k in range(TOP_K):
            _tile_gather_start(yb_hbm, pos_cur, k * tt, tt, ybuf.at[0, k], sem.at[0])
            _tile_gather_start(yb_hbm, pos_nxt, k * tt, tt, ybuf.at[1, k], sem.at[1])

    wait_rows(slot)
    rr = rrow_ref[...]
    acc = x1_ref[...]
    for k in range(TOP_K):
        acc = acc + rr[:, TOP_K + k:TOP_K + k + 1] * _load_token_tiles(ybuf.at[slot, k], tt)
    out_ref[...] = acc * lax.rsqrt(jnp.mean(acc * acc, axis=-1, keepdims=True) + EPS) * fn_ref[...]
    for k in range(TOP_K):
        _tile_gather_start_inline(yb_hbm, pos_far, k * tt, tt, ybuf.at[far, k], sem.at[far])

    @pl.when(i == n - 1)
    def _():
        for s in range(1, RING):
            wait_rows(lax.rem(i + s, RING))


def _combine_call(pos, x1, rrow, yb, fn, *, row_off, n_rows):
    tt = min(COMB_TT, n_rows)
    n_tiles = n_rows // tt
    off = row_off // tt
    d = x1.shape[1]
    pos = pos[row_off * TOP_K:(row_off + n_rows) * TOP_K].reshape(n_tiles, tt, TOP_K)
    pos = pos.transpose(0, 2, 1).reshape(n_tiles, 1, TOP_K * tt)
    pos_spec = lambda f: pl.BlockSpec((None, 1, tt * TOP_K), f, memory_space=pltpu.SMEM)
    return pl.pallas_call(
        functools.partial(_combine_kernel, n=n_tiles),
        grid=(n_tiles,),
        in_specs=[
            pos_spec(lambda i: (i, 0, 0)),
            pos_spec(lambda i: (jnp.minimum(i + 1, n_tiles - 1), 0, 0)),
            pos_spec(lambda i: (jnp.minimum(i + RING - 1, n_tiles - 1), 0, 0)),
            pl.BlockSpec((tt, d), lambda i: (off + i, 0)),
            pl.BlockSpec((tt, LANES), lambda i: (off + i, 0)),
            pl.BlockSpec(memory_space=pl.ANY),
            pl.BlockSpec((1, d), lambda i: (0, 0)),
        ],
        out_specs=pl.BlockSpec((tt, d), lambda i: (i, 0)),
        out_shape=jax.ShapeDtypeStruct((n_rows, d), F32),
        scratch_shapes=[pltpu.VMEM((RING, TOP_K, tt * ROW_TILES, LANES), F32), pltpu.SemaphoreType.DMA((RING,))],
        compiler_params=pltpu.CompilerParams(dimension_semantics=("arbitrary",), vmem_limit_bytes=VMEM_LIMIT),
        name="combine",
    )(pos, pos, pos, x1, rrow, yb, fn)


def kernel(x_prompt, x_sample, cache_conv_a, cache_conv_ssm, state_ssm, norm_mix, w_in, conv_a_w, norm_conv_a, conv_ssm_w, conv_ssm_b, dt_bias, a_log, d_skip, norm_ssm, w_out, norm_ffn, w_router, b_router, w_gate, b_gate, w_up, b_up, w_down, b_down, final_norm):
    bp, lp, _ = x_prompt.shape
    bs, ls, _ = x_sample.shape
    t_p, t_s = bp * lp, bs * ls
    t_all = t_p + t_s
    wts = _prep_mixer_weights(norm_mix, w_in, conv_a_w, norm_conv_a, conv_ssm_w, conv_ssm_b, dt_bias, a_log,
                              d_skip, norm_ssm, w_out, norm_ffn, w_router, b_router)
    zeros = lambda *s: jnp.zeros(s, F32)
    x1, xf, rrow, route_p, ca_p, cs_p, st_p, cnt = _mixer_call(
        x_prompt, zeros(bp, CONV_A_WIDTH - 1, D_CONV), zeros(bp, SSM_CONV_WIDTH - 1, D_XBC),
        zeros(bp, SSM_HEADS, SSM_HEAD_DIM, SSM_STATE), zeros(1, LANES), wts, _mixer_consts(MIX_Q),
        tl=MIX_TL, q=MIX_Q, t_all=t_all, row_off=0)
    x1, xf, rrow, route_s, ca_s, cs_s, st_s, cnt = _mixer_call(
        x_sample, cache_conv_a[0], cache_conv_ssm[0], state_ssm[0], cnt, wts,
        _mixer_consts(ls), tl=ls, q=ls, t_all=t_all, row_off=t_p, alias=(x1, xf, rrow))

    def rows(r0):
        return jnp.concatenate([route_p[:, r0:r0 + TOP_K, :].transpose(0, 2, 1).reshape(t_p, TOP_K),
                                route_s[:, r0:r0 + TOP_K, :].transpose(0, 2, 1).reshape(t_s, TOP_K)],
                               axis=0).astype(jnp.int32)
    slot_tok, pos, items = _routing(rows(0), rows(2 * TOP_K), cnt[0, :N_EXPERTS].astype(jnp.int32))

    yb = _expert_call(slot_tok, items, xf, w_gate[0], b_gate[0], w_up[0], b_up[0], w_down[0], b_down[0])
    fn = final_norm[None]
    y_p = _combine_call(pos, x1, rrow, yb, fn, row_off=0, n_rows=t_p)
    y_s = _combine_call(pos, x1, rrow, yb, fn, row_off=t_p, n_rows=t_s)
    return (y_p.reshape(bp, lp, D_MODEL), y_s.reshape(bs, ls, D_MODEL),
            ca_p[None], cs_p[None], st_p[None], ca_s[None], cs_s[None], st_s[None])
```

```python
import functools

import jax
import jax.numpy as jnp
from jax import lax
from jax.experimental import pallas as pl
from jax.experimental.pallas import tpu as pltpu

F32 = jnp.float32
BF16 = jnp.bfloat16

D_MODEL = 1024
D_CONV = 1024
D_SSM = 1024
CONV_GROUPS = 16
CONV_A_WIDTH = 3
SSM_HEAD_DIM = 64
SSM_HEADS = 16
SSM_GROUPS = 2
SSM_STATE = 128
SSM_CONV_WIDTH = 4
D_XBC = D_SSM + 2 * SSM_GROUPS * SSM_STATE
N_EXPERTS = 32
TOP_K = 4
SWIGLU_LIMIT = 7.0
SWIGLU_ALPHA = 1.702
MOE_BLOCK = 256
EPS = 1e-5

LANES = 128
COL_A = 3 * D_CONV
COL_S = D_SSM + D_XBC + LANES
D_IN_PAD = COL_A + COL_S
assert 2 * SSM_HEAD_DIM == LANES and SSM_STATE == LANES
NEG_BIG = -1e30
VMEM_LIMIT = 56 * 1024 * 1024


def _softplus(v):
    return jnp.maximum(v, 0.0) + jnp.log1p(jnp.exp(-jnp.abs(v)))


def _silu(v):
    return v * jax.nn.sigmoid(v)


def _split2(v):
    hi = v.astype(BF16)
    mid = (v - hi.astype(F32)).astype(BF16)
    return hi, mid


ROUTE_ROWS = 16
ROW_TILES = D_MODEL // LANES


def _store_token_tiles(ref, v, row0=0):
    r = v.shape[0]
    for s in range(ROW_TILES):
        ref[pl.ds(row0 * ROW_TILES + s, r, stride=ROW_TILES), :] = v[:, s * LANES:(s + 1) * LANES]


def _load_token_tiles(ref, r):
    return jnp.concatenate([ref[pl.ds(s, r, stride=ROW_TILES), :] for s in range(ROW_TILES)], axis=1)


def _transpose_rows(v):
    r = v.shape[0]
    if r % LANES:
        v = jnp.concatenate([v, jnp.zeros((LANES - r % LANES, v.shape[1]), v.dtype)], axis=0)
    return v.T[:, :r]


def _ssd_chunk(xs, bm, cm, dt, a_neg, z, dskip, s_scr, tril, e2):
    q = xs.shape[0]
    da = dt * a_neg
    d_hi = da.astype(BF16)
    r1 = da - d_hi.astype(F32)
    d_mid = r1.astype(BF16)
    d_lo = (r1 - d_mid.astype(F32)).astype(BF16)
    cs = jnp.dot(tril, jnp.concatenate([d_hi, d_mid, d_lo], axis=1), preferred_element_type=F32)
    acum = cs[:, 0:LANES] + cs[:, LANES:2 * LANES] + cs[:, 2 * LANES:3 * LANES]
    acum_t = _transpose_rows(acum)
    total = acum[q - 1:q, :]
    stack = jnp.concatenate(
        [dt, jnp.exp(total - acum), jnp.exp(acum), jnp.broadcast_to(jnp.exp(total), (8, LANES))], axis=0)
    s_hi, s_mid = _split2(stack)
    ex = jnp.dot(jnp.concatenate([s_hi, s_mid], axis=1), e2, preferred_element_type=F32)
    dt_e, dec_e, ea_e, cd_e = ex[0:q], ex[q:2 * q], ex[2 * q:3 * q], ex[3 * q:3 * q + 1]
    xdt = xs * dt_e
    xdt_b = xdt.astype(BF16)
    xdec_b = (xdt * dec_e).astype(BF16)
    row = lax.broadcasted_iota(jnp.int32, (q, q), 0)
    col = lax.broadcasted_iota(jnp.int32, (q, q), 1)
    causal = row >= col
    lane = lax.broadcasted_iota(jnp.int32, (q, LANES), 1)
    gw = D_SSM // SSM_GROUPS
    hpg = SSM_HEADS // SSM_GROUPS
    ys = []
    for g in range(SSM_GROUPS):
        cg = cm[:, g * SSM_STATE:(g + 1) * SSM_STATE].astype(BF16)
        bg = bm[:, g * SSM_STATE:(g + 1) * SSM_STATE].astype(BF16)
        cb = lax.dot_general(cg, bg, (((1,), (1,)), ((), ())), preferred_element_type=F32)
        s_g = s_scr[g]
        y_off = jnp.dot(cg, s_g.astype(BF16), preferred_element_type=F32) * ea_e[:, g * gw:(g + 1) * gw]
        pieces = []
        for j in range(hpg // 2):
            h0 = g * hpg + 2 * j
            ms = []
            for h in (h0, h0 + 1):
                diff = acum[:, h:h + 1] - acum_t[h:h + 1, :]
                ms.append(cb * jnp.exp(jnp.where(causal, diff, NEG_BIG)))
            m = jnp.concatenate(ms, axis=1).astype(BF16)
            xp = xdt_b[:, h0 * SSM_HEAD_DIM:(h0 + 2) * SSM_HEAD_DIM]
            zero = jnp.zeros_like(xp)
            rhs = jnp.concatenate([jnp.where(lane < SSM_HEAD_DIM, xp, zero),
                                   jnp.where(lane >= SSM_HEAD_DIM, xp, zero)], axis=0)
            pieces.append(jnp.dot(m, rhs, preferred_element_type=F32))
        ys.append(jnp.concatenate(pieces, axis=1) + y_off)
        upd = lax.dot_general(bg, xdec_b[:, g * gw:(g + 1) * gw], (((0,), (0,)), ((), ())),
                              preferred_element_type=F32)
        s_scr[g] = s_g * cd_e[:, g * gw:(g + 1) * gw] + upd
    y = jnp.concatenate(ys, axis=1) + dskip * xs
    return y * _silu(z)


def _mixer_kernel(*refs, tl, q, nl, n_alias):
    (x_ref, ca0_ref, cs0_ref, ssm0_ref, cnt0_ref, nmix_ref, win_ref, caw_ref, na_ref, csw_ref, csb_ref,
     dtb_ref, alog_ref, dsk_ref, ns_ref, wout_ref, nffn_ref, wr_ref, br_ref,
     tril_ref, e2_ref, g_ref, e3_ref) = refs[:23]
    refs = refs[23 + n_alias:]
    (x1_ref, xf_ref, rrow_ref, route_ref, ca_ref, cs_ref, ssm_ref, cnt_ref,
     upad, xpad, s_scr, ca_scr, cs_scr, cnt_scr) = refs
    l = pl.program_id(1)
    last = nl - 1
    pa_rows = CONV_A_WIDTH - 1
    ps_rows = SSM_CONV_WIDTH - 1

    @pl.when(jnp.logical_and(pl.program_id(0) == 0, l == 0))
    def _():
        cnt_scr[...] = cnt0_ref[...]

    @pl.when(l == 0)
    def _():
        ca_scr[...] = ca0_ref[...]
        cs_scr[...] = cs0_ref[...]
        for hp in range(SSM_HEADS // 2):
            g, j = divmod(hp, SSM_HEADS // SSM_GROUPS // 2)
            pair = jnp.concatenate([ssm0_ref[2 * hp], ssm0_ref[2 * hp + 1]], axis=0)
            s_scr[g, :, j * LANES:(j + 1) * LANES] = pair.T

    caw = caw_ref[...]
    csw = csw_ref[...]
    a_neg = -jnp.exp(alog_ref[...])
    gam = jnp.concatenate([na_ref[...], ns_ref[...]], axis=1)
    nb = D_SSM + SSM_GROUPS * SSM_STATE
    lane = lax.broadcasted_iota(jnp.int32, (q, LANES), 1)
    lane_f = lane.astype(F32)
    tail_a = ca_scr[...]
    tail_s = cs_scr[...]
    for j in range(tl // q):
        rows = slice(j * q, (j + 1) * q)
        x = x_ref[rows, :]
        xn = (x * lax.rsqrt(jnp.mean(x * x, axis=-1, keepdims=True) + EPS) * nmix_ref[...]).astype(BF16)

        pa = jnp.dot(xn, win_ref[:, 0:COL_A], preferred_element_type=F32)
        a_b, a_c, a_h = pa[:, 0:D_CONV], pa[:, D_CONV:2 * D_CONV], pa[:, 2 * D_CONV:3 * D_CONV]
        u = a_c * a_h
        upad[j, 8 - pa_rows:8, :] = tail_a
        upad[j, 8:8 + q, :] = u
        conv = upad[j, 8 - pa_rows:8 - pa_rows + q, :] * caw[0:1, :]
        for k in range(1, CONV_A_WIDTH):
            conv = conv + upad[j, 8 - pa_rows + k:8 - pa_rows + k + q, :] * caw[k:k + 1, :]
        y_a = a_b * conv
        tail_a = u[q - pa_rows:q, :]

        ps = jnp.dot(xn, win_ref[:, COL_A:D_IN_PAD], preferred_element_type=F32)
        z = ps[:, 0:D_SSM]
        xbc = ps[:, D_SSM:D_SSM + D_XBC]
        xpad[j, 8 - ps_rows:8, :] = tail_s
        xpad[j, 8:8 + q, :] = xbc
        dt = _softplus(ps[:, D_SSM + D_XBC:COL_S] + dtb_ref[...])
        acc = xpad[j, 8 - ps_rows:8 - ps_rows + q, :] * csw[0:1, :]
        for k in range(1, SSM_CONV_WIDTH):
            acc = acc + xpad[j, 8 - ps_rows + k:8 - ps_rows + k + q, :] * csw[k:k + 1, :]
        xc = _silu(acc + csb_ref[...])
        tail_s = xbc[q - ps_rows:q, :]
        y_s = _ssd_chunk(xc[:, 0:D_SSM], xc[:, D_SSM:nb], xc[:, nb:D_XBC], dt, a_neg, z, dsk_ref[...],
                         s_scr, tril_ref[...], e2_ref[...])

        mixed = jnp.concatenate([y_a, y_s], axis=1)
        mean = jnp.dot((mixed * mixed).astype(BF16), g_ref[...], preferred_element_type=F32)
        r_hi, r_mid = _split2(lax.rsqrt(mean + EPS))
        rs_e = jnp.dot(jnp.concatenate([r_hi, r_mid], axis=1), e3_ref[...], preferred_element_type=F32)
        mixed_n = (mixed * rs_e * gam).astype(BF16)
        x1 = x + jnp.dot(mixed_n, wout_ref[...], preferred_element_type=F32)
        x1_ref[rows, :] = x1

        xf = x1 * lax.rsqrt(jnp.mean(x1 * x1, axis=-1, keepdims=True) + EPS) * nffn_ref[...]
        _store_token_tiles(xf_ref, xf, row0=j * q)
        f_hi, f_mid = _split2(xf)
        logits = jnp.dot(jnp.concatenate([f_hi, f_hi, f_mid], axis=1), wr_ref[...],
                         preferred_element_type=F32) + br_ref[...]
        work = logits
        vals, idxs = [], []
        for _ in range(TOP_K):
            m = jnp.max(work, axis=-1, keepdims=True)
            idx = jnp.min(jnp.where(work == m, lane_f, float(LANES)), axis=-1, keepdims=True)
            vals.append(m)
            idxs.append(idx)
            work = jnp.where(lane_f == idx, -jnp.inf, work)
        es = [jnp.exp(v - vals[0]) for v in vals]
        inv = 1.0 / (es[0] + es[1] + es[2] + es[3])
        hots = [(lane_f == idxs[k]).astype(F32) for k in range(TOP_K)]
        hot = hots[0] + hots[1] + hots[2] + hots[3]
        before = jnp.dot(tril_ref[...], hot.astype(BF16), preferred_element_type=F32) - hot + cnt_scr[...]
        cnt_scr[...] = before[q - 1:q, :] + hot[q - 1:q, :]
        route = jnp.zeros((q, LANES), F32)
        for k in range(TOP_K):
            route = jnp.where(lane == k, idxs[k], route)
            route = jnp.where(lane == TOP_K + k, es[k] * inv, route)
            route = jnp.where(lane == 2 * TOP_K + k, jnp.sum(hots[k] * before, axis=-1, keepdims=True), route)
        rrow_ref[rows, :] = route
        route_ref[:, rows] = _transpose_rows(route)[0:ROUTE_ROWS, :]

    ca_scr[...] = tail_a
    cs_scr[...] = tail_s
    cnt_ref[...] = cnt_scr[...]

    @pl.when(l == last)
    def _():
        ca_ref[...] = tail_a
        cs_ref[...] = tail_s
        for hp in range(SSM_HEADS // 2):
            g, j = divmod(hp, SSM_HEADS // SSM_GROUPS // 2)
            pair = s_scr[g, :, j * LANES:(j + 1) * LANES].T
            ssm_ref[2 * hp] = pair[0:SSM_HEAD_DIM, :]
            ssm_ref[2 * hp + 1] = pair[SSM_HEAD_DIM:2 * SSM_HEAD_DIM, :]


def _const_spec(shape):
    nd = len(shape)
    return pl.BlockSpec(shape, lambda b, l: (0,) * nd, pipeline_mode=pl.Buffered(1))


def _mixer_call(x, ca0, cs0, ssm0, cnt0, wts, consts, *, tl, q, t_all, row_off, alias=None):
    b_sz, l_sz, _ = x.shape
    nl = l_sz // tl
    blk_off = row_off // tl
    n_alias = 0 if alias is None else len(alias)
    in_specs = [
        pl.BlockSpec((None, tl, D_MODEL), lambda b, l: (b, l, 0)),
        pl.BlockSpec((None, CONV_A_WIDTH - 1, D_CONV), lambda b, l: (b, 0, 0)),
        pl.BlockSpec((None, SSM_CONV_WIDTH - 1, D_XBC), lambda b, l: (b, 0, 0)),
        pl.BlockSpec((None, SSM_HEADS, SSM_HEAD_DIM, SSM_STATE), lambda b, l: (b, 0, 0, 0)),
        _const_spec(cnt0.shape),
    ] + [_const_spec(w.shape) for w in wts] + [_const_spec(c.shape) for c in consts]
    in_specs += [pl.BlockSpec(memory_space=pl.ANY)] * n_alias
    row_spec = pl.BlockSpec((tl, D_MODEL), lambda b, l: (blk_off + b * nl + l, 0))
    out_specs = [
        row_spec,
        pl.BlockSpec((tl * ROW_TILES, LANES), lambda b, l: (blk_off + b * nl + l, 0)),
        pl.BlockSpec((tl, LANES), lambda b, l: (blk_off + b * nl + l, 0)),
        pl.BlockSpec((None, ROUTE_ROWS, tl), lambda b, l: (b, 0, l)),
        pl.BlockSpec((None, CONV_A_WIDTH - 1, D_CONV), lambda b, l: (b, 0, 0)),
        pl.BlockSpec((None, SSM_CONV_WIDTH - 1, D_XBC), lambda b, l: (b, 0, 0)),
        pl.BlockSpec((None, SSM_HEADS, SSM_HEAD_DIM, SSM_STATE), lambda b, l: (b, 0, 0, 0)),
        pl.BlockSpec((1, LANES), lambda b, l: (0, 0)),
    ]
    out_shape = [
        jax.ShapeDtypeStruct((t_all, D_MODEL), F32),
        jax.ShapeDtypeStruct((t_all * ROW_TILES, LANES), F32),
        jax.ShapeDtypeStruct((t_all, LANES), F32),
        jax.ShapeDtypeStruct((b_sz, ROUTE_ROWS, l_sz), F32),
        jax.ShapeDtypeStruct((b_sz, CONV_A_WIDTH - 1, D_CONV), F32),
        jax.ShapeDtypeStruct((b_sz, SSM_CONV_WIDTH - 1, D_XBC), F32),
        jax.ShapeDtypeStruct((b_sz, SSM_HEADS, SSM_HEAD_DIM, SSM_STATE), F32),
        jax.ShapeDtypeStruct((1, LANES), F32),
    ]
    n_in = 5 + len(wts) + len(consts)
    aliases = {n_in + i: i for i in range(n_alias)}
    args = [x, ca0, cs0, ssm0, cnt0, *wts, *consts] + ([] if alias is None else list(alias))
    return pl.pallas_call(
        functools.partial(_mixer_kernel, tl=tl, q=q, nl=nl, n_alias=n_alias),
        grid=(b_sz, nl),
        in_specs=in_specs,
        out_specs=out_specs,
        out_shape=out_shape,
        scratch_shapes=[
            pltpu.VMEM((tl // q, q + 8, D_CONV), F32),
            pltpu.VMEM((tl // q, q + 8, D_XBC), F32),
            pltpu.VMEM((SSM_GROUPS, SSM_STATE, D_SSM // SSM_GROUPS), F32),
            pltpu.VMEM((CONV_A_WIDTH - 1, D_CONV), F32),
            pltpu.VMEM((SSM_CONV_WIDTH - 1, D_XBC), F32),
            pltpu.VMEM((1, LANES), F32),
        ],
        input_output_aliases=aliases,
        compiler_params=pltpu.CompilerParams(
            dimension_semantics=("arbitrary", "arbitrary"), vmem_limit_bytes=VMEM_LIMIT),
        name="mixer",
    )(*args)


def _mixer_consts(q):
    r = jnp.arange(q)
    tril = (r[:, None] >= r[None, :]).astype(BF16)
    j2 = jnp.arange(2 * LANES) % LANES
    e2 = (j2[:, None] == (jnp.arange(D_SSM) // SSM_HEAD_DIM)[None, :]).astype(BF16)
    c = jnp.arange(D_CONV + D_SSM)
    grp = jnp.where(c < D_CONV, c // (D_CONV // CONV_GROUPS),
                    CONV_GROUPS + (c - D_CONV) // (D_SSM // SSM_GROUPS))
    wgt = jnp.where(c < D_CONV, CONV_GROUPS / D_CONV, SSM_GROUPS / D_SSM)
    g = jnp.where(grp[:, None] == jnp.arange(LANES)[None, :], wgt[:, None], 0.0).astype(BF16)
    e3 = (j2[:, None] == grp[None, :]).astype(BF16)
    return tril, e2, g, e3


def _prep_mixer_weights(norm_mix, w_in, conv_a_w, norm_conv_a, conv_ssm_w, conv_ssm_b, dt_bias, a_log,
                        d_skip, norm_ssm, w_out, norm_ffn, w_router, b_router):
    hpad = LANES - SSM_HEADS
    w_in_p = jnp.pad(w_in[0], ((0, 0), (0, hpad))).astype(BF16)
    wr = jnp.pad(w_router[0], ((0, 0), (0, LANES - N_EXPERTS)))
    wr_hi = wr.astype(BF16)
    wr_mid = (wr - wr_hi.astype(F32)).astype(BF16)
    wr3 = jnp.concatenate([wr_hi, wr_mid, wr_hi], axis=0)
    br = jnp.pad(b_router[0], (0, LANES - N_EXPERTS), constant_values=NEG_BIG)[None]
    return (norm_mix[0][None], w_in_p, conv_a_w[0], norm_conv_a[0][None], conv_ssm_w[0],
            conv_ssm_b[0][None], jnp.pad(dt_bias[0], (0, hpad))[None], jnp.pad(a_log[0], (0, hpad))[None],
            jnp.repeat(d_skip[0], SSM_HEAD_DIM)[None], norm_ssm[0][None], w_out[0].astype(BF16),
            norm_ffn[0][None], wr3, br)


MIX_TL = 512
MIX_Q = 256
COMB_TT = 256


def _routing(top_i, rank, counts):
    t = top_i.shape[1]
    n = t * TOP_K
    assert n % MOE_BLOCK == 0
    nblk = n // MOE_BLOCK
    n_items = nblk + N_EXPERTS
    shift = max(n - 1, 1).bit_length()
    assert N_EXPERTS << shift < 2 ** 31
    ent = (jnp.arange(t, dtype=jnp.int32) * TOP_K)[None, :] + jnp.arange(TOP_K, dtype=jnp.int32)[:, None]
    key = lax.sort((top_i * (1 << shift) + ent).reshape(n))
    slot_tok = (key & ((1 << shift) - 1)) // TOP_K
    group_end = jnp.cumsum(counts)
    group_start = group_end - counts
    pos = group_start[top_i] + rank
    bp = lax.sort(jnp.concatenate([jnp.arange(nblk, dtype=jnp.int32) * MOE_BLOCK, group_start]))
    dup = jnp.concatenate([jnp.zeros((1,), bool), bp[1:] == bp[:-1]])
    bp = lax.sort(jnp.where(jnp.logical_or(dup, bp >= n), n, bp))
    blk = jnp.minimum(bp // MOE_BLOCK, nblk - 1)
    lo = bp - blk * MOE_BLOCK
    exp = jnp.sum((group_end[None, :] <= jnp.minimum(bp, n - 1)[:, None]).astype(jnp.int32), axis=1)
    items = tuple(a.astype(jnp.int32) for a in (blk, exp, lo))
    assert items[0].shape == (n_items,)
    return slot_tok.astype(jnp.int32).reshape(nblk, 1, MOE_BLOCK), pos.astype(jnp.int32), items


def _tile_gather_start(src_hbm, idx_ref, idx_off, n_rows, dst, sem):
    def body(i, carry):
        src_row = pl.multiple_of(idx_ref[0, idx_off + i] * ROW_TILES, ROW_TILES)
        dst_row = pl.multiple_of(i * ROW_TILES, ROW_TILES)
        pltpu.make_async_copy(src_hbm.at[pl.ds(src_row, ROW_TILES), :],
                              dst.at[pl.ds(dst_row, ROW_TILES), :], sem).start()
        return carry
    lax.fori_loop(0, n_rows, body, 0, unroll=8)


def _tile_gather_start_inline(src_hbm, idx_ref, idx_off, n_rows, dst, sem):
    for i in range(n_rows):
        src_row = pl.multiple_of(idx_ref[0, idx_off + i] * ROW_TILES, ROW_TILES)
        pltpu.make_async_copy(src_hbm.at[pl.ds(src_row, ROW_TILES), :],
                              dst.at[pl.ds(i * ROW_TILES, ROW_TILES), :], sem).start(priority=i % 2)


RING = 3


def _expert_kernel(blk_ref, exp_ref, lo_ref, tok_cur, tok_nxt, tok_far, xf_hbm,
                   wg_ref, bg_ref, wu_ref, bu_ref, wd_ref, bd_ref, yb_ref, xbuf, sem, wgb, wub, wdb, *, n_items):
    w = pl.program_id(0)
    lo = lo_ref[w]
    slot = lax.rem(w, RING)
    far = lax.rem(w + RING - 1, RING)

    def wait_rows(s):
        pltpu.make_async_copy(xf_hbm.at[pl.ds(0, MOE_BLOCK * ROW_TILES), :], xbuf.at[s], sem.at[s]).wait()

    @pl.when(w == 0)
    def _():
        _tile_gather_start(xf_hbm, tok_cur, 0, MOE_BLOCK, xbuf.at[0], sem.at[0])
        _tile_gather_start(xf_hbm, tok_nxt, 0, MOE_BLOCK, xbuf.at[1], sem.at[1])

    @pl.when(jnp.logical_or(w == 0, exp_ref[w] != exp_ref[jnp.maximum(w - 1, 0)]))
    def _():
        wgb[...] = wg_ref[...].astype(BF16)
        wub[...] = wu_ref[...].astype(BF16)
        wdb[...] = wd_ref[...].astype(BF16)

    wait_rows(slot)
    xb = _load_token_tiles(xbuf.at[slot], MOE_BLOCK).astype(BF16)
    gate = jnp.dot(xb, wgb[...], preferred_element_type=F32) + bg_ref[...]
    up = jnp.dot(xb, wub[...], preferred_element_type=F32) + bu_ref[...]
    gate = jnp.minimum(gate, SWIGLU_LIMIT)
    up = jnp.clip(up, -SWIGLU_LIMIT, SWIGLU_LIMIT)
    glu = gate * jax.nn.sigmoid(gate * SWIGLU_ALPHA)
    h = ((up + 1.0) * glu).astype(BF16)
    y = jnp.dot(h, wdb[...], preferred_element_type=F32) + bd_ref[...]
    keep = lax.broadcasted_iota(jnp.int32, (MOE_BLOCK, D_MODEL), 0) < lo
    _store_token_tiles(yb_ref, jnp.where(keep, _load_token_tiles(yb_ref, MOE_BLOCK), y))
    _tile_gather_start_inline(xf_hbm, tok_far, 0, MOE_BLOCK, xbuf.at[far], sem.at[far])

    @pl.when(w == n_items - 1)
    def _():
        for s in range(1, RING):
            wait_rows(lax.rem(w + s, RING))


def _expert_call(slot_tok, items, xf, w_gate, b_gate, w_up, b_up, w_down, b_down):
    nblk = slot_tok.shape[0]
    n_items = items[0].shape[0]
    d, de = w_gate.shape[1], w_gate.shape[2]
    tok_spec = lambda f: pl.BlockSpec((None, 1, MOE_BLOCK), f, memory_space=pltpu.SMEM)
    w_spec = lambda r, c: pl.BlockSpec((None, r, c), lambda w, blk, exp, lo: (exp[w], 0, 0))
    grid_spec = pltpu.PrefetchScalarGridSpec(
        num_scalar_prefetch=3,
        grid=(n_items,),
        in_specs=[
            tok_spec(lambda w, blk, exp, lo: (blk[w], 0, 0)),
            tok_spec(lambda w, blk, exp, lo: (blk[jnp.minimum(w + 1, n_items - 1)], 0, 0)),
            tok_spec(lambda w, blk, exp, lo: (blk[jnp.minimum(w + RING - 1, n_items - 1)], 0, 0)),
            pl.BlockSpec(memory_space=pl.ANY),
            w_spec(d, de), w_spec(1, de), w_spec(d, de), w_spec(1, de), w_spec(de, d), w_spec(1, d),
        ],
        out_specs=pl.BlockSpec((MOE_BLOCK * ROW_TILES, LANES), lambda w, blk, exp, lo: (blk[w], 0)),
        scratch_shapes=[
            pltpu.VMEM((RING, MOE_BLOCK * ROW_TILES, LANES), F32),
            pltpu.SemaphoreType.DMA((RING,)),
            pltpu.VMEM((d, de), BF16), pltpu.VMEM((d, de), BF16), pltpu.VMEM((de, d), BF16),
        ],
    )
    return pl.pallas_call(
        functools.partial(_expert_kernel, n_items=n_items),
        grid_spec=grid_spec,
        out_shape=jax.ShapeDtypeStruct((nblk * MOE_BLOCK * ROW_TILES, LANES), F32),
        compiler_params=pltpu.CompilerParams(dimension_semantics=("arbitrary",), vmem_limit_bytes=VMEM_LIMIT),
        name="experts",
    )(*items, slot_tok, slot_tok, slot_tok, xf, w_gate, b_gate[:, None, :], w_up, b_up[:, None, :],
      w_down, b_down[:, None, :])


def _combine_kernel(pos_cur, pos_nxt, pos_far, x1_ref, rrow_ref, yb_hbm, fn_ref, out_ref, ybuf, sem, *, n):
    i = pl.program_id(0)
    slot = lax.rem(i, RING)
    far = lax.rem(i + RING - 1, RING)
    tt = x1_ref.shape[0]

    def wait_rows(s):
        for k in range(TOP_K):
            pltpu.make_async_copy(yb_hbm.at[pl.ds(0, tt * ROW_TILES), :], ybuf.at[s, k], sem.at[s]).wait()

    @pl.when(i == 0)
    def _():
        for k in range(TOP_K):
            _tile_gather_start(yb_hbm, pos_cur, k * tt, tt, ybuf.at[0, k], sem.at[0])
            _tile_gather_start(yb_hbm, pos_nxt, k * tt, tt, ybuf.at[1, k], sem.at[1])

    wait_rows(slot)
    rr = rrow_ref[...]
    acc = x1_ref[...]
    for k in range(TOP_K):
        acc = acc + rr[:, TOP_K + k:TOP_K + k + 1] * _load_token_tiles(ybuf.at[slot, k], tt)
    out_ref[...] = acc * lax.rsqrt(jnp.mean(acc * acc, axis=-1, keepdims=True) + EPS) * fn_ref[...]
    for k in range(TOP_K):
        _tile_gather_start_inline(yb_hbm, pos_far, k * tt, tt, ybuf.at[far, k], sem.at[far])

    @pl.when(i == n - 1)
    def _():
        for s in range(1, RING):
            wait_rows(lax.rem(i + s, RING))


def _combine_call(pos, x1, rrow, yb, fn, *, row_off, n_rows):
    tt = min(COMB_TT, n_rows)
    n_tiles = n_rows // tt
    off = row_off // tt
    d = x1.shape[1]
    pos = pos[:, row_off:row_off + n_rows].reshape(TOP_K, n_tiles, tt)
    pos = pos.transpose(1, 0, 2).reshape(n_tiles, 1, TOP_K * tt)
    pos_spec = lambda f: pl.BlockSpec((None, 1, tt * TOP_K), f, memory_space=pltpu.SMEM)
    return pl.pallas_call(
        functools.partial(_combine_kernel, n=n_tiles),
        grid=(n_tiles,),
        in_specs=[
            pos_spec(lambda i: (i, 0, 0)),
            pos_spec(lambda i: (jnp.minimum(i + 1, n_tiles - 1), 0, 0)),
            pos_spec(lambda i: (jnp.minimum(i + RING - 1, n_tiles - 1), 0, 0)),
            pl.BlockSpec((tt, d), lambda i: (off + i, 0)),
            pl.BlockSpec((tt, LANES), lambda i: (off + i, 0)),
            pl.BlockSpec(memory_space=pl.ANY),
            pl.BlockSpec((1, d), lambda i: (0, 0)),
        ],
        out_specs=pl.BlockSpec((tt, d), lambda i: (i, 0)),
        out_shape=jax.ShapeDtypeStruct((n_rows, d), F32),
        scratch_shapes=[pltpu.VMEM((RING, TOP_K, tt * ROW_TILES, LANES), F32), pltpu.SemaphoreType.DMA((RING,))],
        compiler_params=pltpu.CompilerParams(dimension_semantics=("arbitrary",), vmem_limit_bytes=VMEM_LIMIT),
        name="combine",
    )(pos, pos, pos, x1, rrow, yb, fn)


def kernel(x_prompt, x_sample, cache_conv_a, cache_conv_ssm, state_ssm, norm_mix, w_in, conv_a_w, norm_conv_a, conv_ssm_w, conv_ssm_b, dt_bias, a_log, d_skip, norm_ssm, w_out, norm_ffn, w_router, b_router, w_gate, b_gate, w_up, b_up, w_down, b_down, final_norm):
    bp, lp, _ = x_prompt.shape
    bs, ls, _ = x_sample.shape
    t_p, t_s = bp * lp, bs * ls
    t_all = t_p + t_s
    wts = _prep_mixer_weights(norm_mix, w_in, conv_a_w, norm_conv_a, conv_ssm_w, conv_ssm_b, dt_bias, a_log,
                              d_skip, norm_ssm, w_out, norm_ffn, w_router, b_router)
    zeros = lambda *s: jnp.zeros(s, F32)
    x1, xf, rrow, route_p, ca_p, cs_p, st_p, cnt = _mixer_call(
        x_prompt, zeros(bp, CONV_A_WIDTH - 1, D_CONV), zeros(bp, SSM_CONV_WIDTH - 1, D_XBC),
        zeros(bp, SSM_HEADS, SSM_HEAD_DIM, SSM_STATE), zeros(1, LANES), wts, _mixer_consts(MIX_Q),
        tl=MIX_TL, q=MIX_Q, t_all=t_all, row_off=0)
    x1, xf, rrow, route_s, ca_s, cs_s, st_s, cnt = _mixer_call(
        x_sample, cache_conv_a[0], cache_conv_ssm[0], state_ssm[0], cnt, wts,
        _mixer_consts(ls), tl=ls, q=ls, t_all=t_all, row_off=t_p, alias=(x1, xf, rrow))

    def rows(r0):
        return jnp.concatenate([route_p[:, r0:r0 + TOP_K, :].transpose(1, 0, 2).reshape(TOP_K, t_p),
                                route_s[:, r0:r0 + TOP_K, :].transpose(1, 0, 2).reshape(TOP_K, t_s)],
                               axis=1).astype(jnp.int32)
    slot_tok, pos, items = _routing(rows(0), rows(2 * TOP_K), cnt[0, :N_EXPERTS].astype(jnp.int32))

    yb = _expert_call(slot_tok, items, xf, w_gate[0], b_gate[0], w_up[0], b_up[0], w_down[0], b_down[0])
    fn = final_norm[None]
    y_p = _combine_call(pos, x1, rrow, yb, fn, row_off=0, n_rows=t_p)
    y_s = _combine_call(pos, x1, rrow, yb, fn, row_off=t_p, n_rows=t_s)
    return (y_p.reshape(bp, lp, D_MODEL), y_s.reshape(bs, ls, D_MODEL),
            ca_p[None], cs_p[None], st_p[None], ca_s[None], cs_s[None], st_s[None])
```

```python
import functools

import jax
import jax.numpy as jnp
from jax import lax
from jax.experimental import pallas as pl
from jax.experimental.pallas import tpu as pltpu

F32 = jnp.float32
BF16 = jnp.bfloat16

D_MODEL = 1024
D_CONV = 1024
D_SSM = 1024
CONV_GROUPS = 16
CONV_A_WIDTH = 3
SSM_HEAD_DIM = 64
SSM_HEADS = 16
SSM_GROUPS = 2
SSM_STATE = 128
SSM_CONV_WIDTH = 4
D_XBC = D_SSM + 2 * SSM_GROUPS * SSM_STATE
N_EXPERTS = 32
TOP_K = 4
SWIGLU_LIMIT = 7.0
SWIGLU_ALPHA = 1.702
MOE_BLOCK = 256
EPS = 1e-5

LANES = 128
COL_A = 3 * D_CONV
COL_S = D_SSM + D_XBC + LANES
D_IN_PAD = COL_A + COL_S
assert 2 * SSM_HEAD_DIM == LANES and SSM_STATE == LANES
NEG_BIG = -1e30
VMEM_LIMIT = 56 * 1024 * 1024


def _softplus(v):
    return jnp.maximum(v, 0.0) + jnp.log1p(jnp.exp(-jnp.abs(v)))


def _silu(v):
    return v * jax.nn.sigmoid(v)


def _split2(v):
    hi = v.astype(BF16)
    mid = (v - hi.astype(F32)).astype(BF16)
    return hi, mid


ROUTE_ROWS = 16
ROW_TILES = D_MODEL // LANES


def _store_token_tiles(ref, v, row0=0):
    r = v.shape[0]
    for s in range(ROW_TILES):
        ref[pl.ds(row0 * ROW_TILES + s, r, stride=ROW_TILES), :] = v[:, s * LANES:(s + 1) * LANES]


def _load_token_tiles(ref, r):
    return jnp.concatenate([ref[pl.ds(s, r, stride=ROW_TILES), :] for s in range(ROW_TILES)], axis=1)


def _transpose_rows(v):
    r = v.shape[0]
    if r % LANES:
        v = jnp.concatenate([v, jnp.zeros((LANES - r % LANES, v.shape[1]), v.dtype)], axis=0)
    return v.T[:, :r]


def _ssd_chunk(xs, bm, cm, dt, a_neg, z, dskip, s_scr, tril, e2):
    q = xs.shape[0]
    da = dt * a_neg
    d_hi = da.astype(BF16)
    r1 = da - d_hi.astype(F32)
    d_mid = r1.astype(BF16)
    d_lo = (r1 - d_mid.astype(F32)).astype(BF16)
    cs = jnp.dot(tril, jnp.concatenate([d_hi, d_mid, d_lo], axis=1), preferred_element_type=F32)
    acum = cs[:, 0:LANES] + cs[:, LANES:2 * LANES] + cs[:, 2 * LANES:3 * LANES]
    acum_t = _transpose_rows(acum)
    total = acum[q - 1:q, :]
    stack = jnp.concatenate(
        [dt, jnp.exp(total - acum), jnp.exp(acum), jnp.broadcast_to(jnp.exp(total), (8, LANES))], axis=0)
    s_hi, s_mid = _split2(stack)
    ex = jnp.dot(jnp.concatenate([s_hi, s_mid], axis=1), e2, preferred_element_type=F32)
    dt_e, dec_e, ea_e, cd_e = ex[0:q], ex[q:2 * q], ex[2 * q:3 * q], ex[3 * q:3 * q + 1]
    xdt = xs * dt_e
    xdt_b = xdt.astype(BF16)
    xdec_b = (xdt * dec_e).astype(BF16)
    row = lax.broadcasted_iota(jnp.int32, (q, q), 0)
    col = lax.broadcasted_iota(jnp.int32, (q, q), 1)
    causal = row >= col
    lane = lax.broadcasted_iota(jnp.int32, (q, LANES), 1)
    gw = D_SSM // SSM_GROUPS
    hpg = SSM_HEADS // SSM_GROUPS
    ys = []
    for g in range(SSM_GROUPS):
        cg = cm[:, g * SSM_STATE:(g + 1) * SSM_STATE].astype(BF16)
        bg = bm[:, g * SSM_STATE:(g + 1) * SSM_STATE].astype(BF16)
        cb = lax.dot_general(cg, bg, (((1,), (1,)), ((), ())), preferred_element_type=F32)
        s_g = s_scr[g]
        y_off = jnp.dot(cg, s_g.astype(BF16), preferred_element_type=F32) * ea_e[:, g * gw:(g + 1) * gw]
        pieces = []
        for j in range(hpg // 2):
            h0 = g * hpg + 2 * j
            ms = []
            for h in (h0, h0 + 1):
                diff = acum[:, h:h + 1] - acum_t[h:h + 1, :]
                ms.append(cb * jnp.exp(jnp.where(causal, diff, NEG_BIG)))
            m = jnp.concatenate(ms, axis=1).astype(BF16)
            xp = xdt_b[:, h0 * SSM_HEAD_DIM:(h0 + 2) * SSM_HEAD_DIM]
            zero = jnp.zeros_like(xp)
            rhs = jnp.concatenate([jnp.where(lane < SSM_HEAD_DIM, xp, zero),
                                   jnp.where(lane >= SSM_HEAD_DIM, xp, zero)], axis=0)
            pieces.append(jnp.dot(m, rhs, preferred_element_type=F32))
        ys.append(jnp.concatenate(pieces, axis=1) + y_off)
        upd = lax.dot_general(bg, xdec_b[:, g * gw:(g + 1) * gw], (((0,), (0,)), ((), ())),
                              preferred_element_type=F32)
        s_scr[g] = s_g * cd_e[:, g * gw:(g + 1) * gw] + upd
    y = jnp.concatenate(ys, axis=1) + dskip * xs
    return y * _silu(z)


def _mixer_kernel(*refs, tl, q, nl, n_alias):
    (x_ref, ca0_ref, cs0_ref, ssm0_ref, cnt0_ref, nmix_ref, win_ref, caw_ref, na_ref, csw_ref, csb_ref,
     dtb_ref, alog_ref, dsk_ref, ns_ref, wout_ref, nffn_ref, wr_ref, br_ref,
     tril_ref, e2_ref, g_ref, e3_ref) = refs[:23]
    refs = refs[23 + n_alias:]
    (x1_ref, xf_ref, rrow_ref, route_ref, ca_ref, cs_ref, ssm_ref, cnt_ref,
     upad, xpad, s_scr, ca_scr, cs_scr, cnt_scr) = refs
    l = pl.program_id(1)
    last = nl - 1
    pa_rows = CONV_A_WIDTH - 1
    ps_rows = SSM_CONV_WIDTH - 1

    @pl.when(jnp.logical_and(pl.program_id(0) == 0, l == 0))
    def _():
        cnt_scr[...] = cnt0_ref[...]

    @pl.when(l == 0)
    def _():
        ca_scr[...] = ca0_ref[...]
        cs_scr[...] = cs0_ref[...]
        for hp in range(SSM_HEADS // 2):
            g, j = divmod(hp, SSM_HEADS // SSM_GROUPS // 2)
            pair = jnp.concatenate([ssm0_ref[2 * hp], ssm0_ref[2 * hp + 1]], axis=0)
            s_scr[g, :, j * LANES:(j + 1) * LANES] = pair.T

    caw = caw_ref[...]
    csw = csw_ref[...]
    a_neg = -jnp.exp(alog_ref[...])
    gam = jnp.concatenate([na_ref[...], ns_ref[...]], axis=1)
    nb = D_SSM + SSM_GROUPS * SSM_STATE
    lane = lax.broadcasted_iota(jnp.int32, (q, LANES), 1)
    lane_f = lane.astype(F32)
    tail_a = ca_scr[...]
    tail_s = cs_scr[...]
    for j in range(tl // q):
        rows = slice(j * q, (j + 1) * q)
        x = x_ref[rows, :]
        xn = (x * lax.rsqrt(jnp.mean(x * x, axis=-1, keepdims=True) + EPS) * nmix_ref[...]).astype(BF16)

        pa = jnp.dot(xn, win_ref[:, 0:COL_A], preferred_element_type=F32)
        a_b, a_c, a_h = pa[:, 0:D_CONV], pa[:, D_CONV:2 * D_CONV], pa[:, 2 * D_CONV:3 * D_CONV]
        u = a_c * a_h
        upad[j, 8 - pa_rows:8, :] = tail_a
        upad[j, 8:8 + q, :] = u
        conv = upad[j, 8 - pa_rows:8 - pa_rows + q, :] * caw[0:1, :]
        for k in range(1, CONV_A_WIDTH):
            conv = conv + upad[j, 8 - pa_rows + k:8 - pa_rows + k + q, :] * caw[k:k + 1, :]
        y_a = a_b * conv
        tail_a = u[q - pa_rows:q, :]

        ps = jnp.dot(xn, win_ref[:, COL_A:D_IN_PAD], preferred_element_type=F32)
        z = ps[:, 0:D_SSM]
        xbc = ps[:, D_SSM:D_SSM + D_XBC]
        xpad[j, 8 - ps_rows:8, :] = tail_s
        xpad[j, 8:8 + q, :] = xbc
        dt = _softplus(ps[:, D_SSM + D_XBC:COL_S] + dtb_ref[...])
        acc = xpad[j, 8 - ps_rows:8 - ps_rows + q, :] * csw[0:1, :]
        for k in range(1, SSM_CONV_WIDTH):
            acc = acc + xpad[j, 8 - ps_rows + k:8 - ps_rows + k + q, :] * csw[k:k + 1, :]
        xc = _silu(acc + csb_ref[...])
        tail_s = xbc[q - ps_rows:q, :]
        y_s = _ssd_chunk(xc[:, 0:D_SSM], xc[:, D_SSM:nb], xc[:, nb:D_XBC], dt, a_neg, z, dsk_ref[...],
                         s_scr, tril_ref[...], e2_ref[...])

        mixed = jnp.concatenate([y_a, y_s], axis=1)
        mean = jnp.dot((mixed * mixed).astype(BF16), g_ref[...], preferred_element_type=F32)
        r_hi, r_mid = _split2(lax.rsqrt(mean + EPS))
        rs_e = jnp.dot(jnp.concatenate([r_hi, r_mid], axis=1), e3_ref[...], preferred_element_type=F32)
        mixed_n = (mixed * rs_e * gam).astype(BF16)
        x1 = x + jnp.dot(mixed_n, wout_ref[...], preferred_element_type=F32)
        x1_ref[rows, :] = x1

        xf = x1 * lax.rsqrt(jnp.mean(x1 * x1, axis=-1, keepdims=True) + EPS) * nffn_ref[...]
        _store_token_tiles(xf_ref, xf, row0=j * q)
        f_hi, f_mid = _split2(xf)
        logits = jnp.dot(jnp.concatenate([f_hi, f_hi, f_mid], axis=1), wr_ref[...],
                         preferred_element_type=F32) + br_ref[...]
        work = logits
        vals, idxs = [], []
        for _ in range(TOP_K):
            m = jnp.max(work, axis=-1, keepdims=True)
            idx = jnp.min(jnp.where(work == m, lane_f, float(LANES)), axis=-1, keepdims=True)
            vals.append(m)
            idxs.append(idx)
            work = jnp.where(lane_f == idx, -jnp.inf, work)
        es = [jnp.exp(v - vals[0]) for v in vals]
        inv = 1.0 / (es[0] + es[1] + es[2] + es[3])
        hots = [(lane_f == idxs[k]).astype(F32) for k in range(TOP_K)]
        hot = hots[0] + hots[1] + hots[2] + hots[3]
        before = jnp.dot(tril_ref[...], hot.astype(BF16), preferred_element_type=F32) - hot + cnt_scr[...]
        cnt_scr[...] = before[q - 1:q, :] + hot[q - 1:q, :]
        route = jnp.zeros((q, LANES), F32)
        for k in range(TOP_K):
            route = jnp.where(lane == k, idxs[k], route)
            route = jnp.where(lane == TOP_K + k, es[k] * inv, route)
            route = jnp.where(lane == 2 * TOP_K + k, jnp.sum(hots[k] * before, axis=-1, keepdims=True), route)
        rrow_ref[rows, :] = route
        route_ref[:, rows] = _transpose_rows(route)[0:ROUTE_ROWS, :]

    ca_scr[...] = tail_a
    cs_scr[...] = tail_s
    cnt_ref[...] = cnt_scr[...]

    @pl.when(l == last)
    def _():
        ca_ref[...] = tail_a
        cs_ref[...] = tail_s
        for hp in range(SSM_HEADS // 2):
            g, j = divmod(hp, SSM_HEADS // SSM_GROUPS // 2)
            pair = s_scr[g, :, j * LANES:(j + 1) * LANES].T
            ssm_ref[2 * hp] = pair[0:SSM_HEAD_DIM, :]
            ssm_ref[2 * hp + 1] = pair[SSM_HEAD_DIM:2 * SSM_HEAD_DIM, :]


def _const_spec(shape):
    nd = len(shape)
    return pl.BlockSpec(shape, lambda b, l: (0,) * nd, pipeline_mode=pl.Buffered(1))


def _mixer_call(x, ca0, cs0, ssm0, cnt0, wts, consts, *, tl, q, t_all, row_off, alias=None):
    b_sz, l_sz, _ = x.shape
    nl = l_sz // tl
    blk_off = row_off // tl
    n_alias = 0 if alias is None else len(alias)
    in_specs = [
        pl.BlockSpec((None, tl, D_MODEL), lambda b, l: (b, l, 0)),
        pl.BlockSpec((None, CONV_A_WIDTH - 1, D_CONV), lambda b, l: (b, 0, 0)),
        pl.BlockSpec((None, SSM_CONV_WIDTH - 1, D_XBC), lambda b, l: (b, 0, 0)),
        pl.BlockSpec((None, SSM_HEADS, SSM_HEAD_DIM, SSM_STATE), lambda b, l: (b, 0, 0, 0)),
        _const_spec(cnt0.shape),
    ] + [_const_spec(w.shape) for w in wts] + [_const_spec(c.shape) for c in consts]
    in_specs += [pl.BlockSpec(memory_space=pl.ANY)] * n_alias
    row_spec = pl.BlockSpec((tl, D_MODEL), lambda b, l: (blk_off + b * nl + l, 0))
    out_specs = [
        row_spec,
        pl.BlockSpec((tl * ROW_TILES, LANES), lambda b, l: (blk_off + b * nl + l, 0)),
        pl.BlockSpec((tl, LANES), lambda b, l: (blk_off + b * nl + l, 0)),
        pl.BlockSpec((None, ROUTE_ROWS, tl), lambda b, l: (b, 0, l)),
        pl.BlockSpec((None, CONV_A_WIDTH - 1, D_CONV), lambda b, l: (b, 0, 0)),
        pl.BlockSpec((None, SSM_CONV_WIDTH - 1, D_XBC), lambda b, l: (b, 0, 0)),
        pl.BlockSpec((None, SSM_HEADS, SSM_HEAD_DIM, SSM_STATE), lambda b, l: (b, 0, 0, 0)),
        pl.BlockSpec((1, LANES), lambda b, l: (0, 0)),
    ]
    out_shape = [
        jax.ShapeDtypeStruct((t_all, D_MODEL), F32),
        jax.ShapeDtypeStruct((t_all * ROW_TILES, LANES), F32),
        jax.ShapeDtypeStruct((t_all, LANES), F32),
        jax.ShapeDtypeStruct((b_sz, ROUTE_ROWS, l_sz), F32),
        jax.ShapeDtypeStruct((b_sz, CONV_A_WIDTH - 1, D_CONV), F32),
        jax.ShapeDtypeStruct((b_sz, SSM_CONV_WIDTH - 1, D_XBC), F32),
        jax.ShapeDtypeStruct((b_sz, SSM_HEADS, SSM_HEAD_DIM, SSM_STATE), F32),
        jax.ShapeDtypeStruct((1, LANES), F32),
    ]
    n_in = 5 + len(wts) + len(consts)
    aliases = {n_in + i: i for i in range(n_alias)}
    args = [x, ca0, cs0, ssm0, cnt0, *wts, *consts] + ([] if alias is None else list(alias))
    return pl.pallas_call(
        functools.partial(_mixer_kernel, tl=tl, q=q, nl=nl, n_alias=n_alias),
        grid=(b_sz, nl),
        in_specs=in_specs,
        out_specs=out_specs,
        out_shape=out_shape,
        scratch_shapes=[
            pltpu.VMEM((tl // q, q + 8, D_CONV), F32),
            pltpu.VMEM((tl // q, q + 8, D_XBC), F32),
            pltpu.VMEM((SSM_GROUPS, SSM_STATE, D_SSM // SSM_GROUPS), F32),
            pltpu.VMEM((CONV_A_WIDTH - 1, D_CONV), F32),
            pltpu.VMEM((SSM_CONV_WIDTH - 1, D_XBC), F32),
            pltpu.VMEM((1, LANES), F32),
        ],
        input_output_aliases=aliases,
        compiler_params=pltpu.CompilerParams(
            dimension_semantics=("arbitrary", "arbitrary"), vmem_limit_bytes=VMEM_LIMIT),
        name="mixer",
    )(*args)


def _mixer_consts(q):
    r = jnp.arange(q)
    tril = (r[:, None] >= r[None, :]).astype(BF16)
    j2 = jnp.arange(2 * LANES) % LANES
    e2 = (j2[:, None] == (jnp.arange(D_SSM) // SSM_HEAD_DIM)[None, :]).astype(BF16)
    c = jnp.arange(D_CONV + D_SSM)
    grp = jnp.where(c < D_CONV, c // (D_CONV // CONV_GROUPS),
                    CONV_GROUPS + (c - D_CONV) // (D_SSM // SSM_GROUPS))
    wgt = jnp.where(c < D_CONV, CONV_GROUPS / D_CONV, SSM_GROUPS / D_SSM)
    g = jnp.where(grp[:, None] == jnp.arange(LANES)[None, :], wgt[:, None], 0.0).astype(BF16)
    e3 = (j2[:, None] == grp[None, :]).astype(BF16)
    return tril, e2, g, e3


def _prep_mixer_weights(norm_mix, w_in, conv_a_w, norm_conv_a, conv_ssm_w, conv_ssm_b, dt_bias, a_log,
                        d_skip, norm_ssm, w_out, norm_ffn, w_router, b_router):
    hpad = LANES - SSM_HEADS
    w_in_p = jnp.pad(w_in[0], ((0, 0), (0, hpad))).astype(BF16)
    wr = jnp.pad(w_router[0], ((0, 0), (0, LANES - N_EXPERTS)))
    wr_hi = wr.astype(BF16)
    wr_mid = (wr - wr_hi.astype(F32)).astype(BF16)
    wr3 = jnp.concatenate([wr_hi, wr_mid, wr_hi], axis=0)
    br = jnp.pad(b_router[0], (0, LANES - N_EXPERTS), constant_values=NEG_BIG)[None]
    return (norm_mix[0][None], w_in_p, conv_a_w[0], norm_conv_a[0][None], conv_ssm_w[0],
            conv_ssm_b[0][None], jnp.pad(dt_bias[0], (0, hpad))[None], jnp.pad(a_log[0], (0, hpad))[None],
            jnp.repeat(d_skip[0], SSM_HEAD_DIM)[None], norm_ssm[0][None], w_out[0].astype(BF16),
            norm_ffn[0][None], wr3, br)


MIX_TL = 512
MIX_Q = 256
COMB_TT = 256


def _routing(top_i, rank, counts):
    t = top_i.shape[1]
    n = t * TOP_K
    assert n % MOE_BLOCK == 0
    nblk = n // MOE_BLOCK
    n_items = nblk + N_EXPERTS
    shift = max(n - 1, 1).bit_length()
    assert N_EXPERTS << shift < 2 ** 31
    ent = (jnp.arange(t, dtype=jnp.int32) * TOP_K)[None, :] + jnp.arange(TOP_K, dtype=jnp.int32)[:, None]
    key = lax.sort((top_i * (1 << shift) + ent).reshape(n))
    slot_tok = (key & ((1 << shift) - 1)) // TOP_K
    group_end = jnp.cumsum(counts)
    group_start = group_end - counts
    pos = rank
    for e in range(N_EXPERTS):
        pos = pos + jnp.where(top_i == e, group_start[e], 0)
    bp = lax.sort(jnp.concatenate([jnp.arange(nblk, dtype=jnp.int32) * MOE_BLOCK, group_start]))
    dup = jnp.concatenate([jnp.zeros((1,), bool), bp[1:] == bp[:-1]])
    bp = lax.sort(jnp.where(jnp.logical_or(dup, bp >= n), n, bp))
    blk = jnp.minimum(bp // MOE_BLOCK, nblk - 1)
    lo = bp - blk * MOE_BLOCK
    exp = jnp.sum((group_end[None, :] <= jnp.minimum(bp, n - 1)[:, None]).astype(jnp.int32), axis=1)
    items = tuple(a.astype(jnp.int32) for a in (blk, exp, lo))
    assert items[0].shape == (n_items,)
    return slot_tok.astype(jnp.int32).reshape(nblk, 1, MOE_BLOCK), pos.astype(jnp.int32), items


def _tile_gather_start(src_hbm, idx_ref, idx_off, n_rows, dst, sem):
    def body(i, carry):
        src_row = pl.multiple_of(idx_ref[0, idx_off + i] * ROW_TILES, ROW_TILES)
        dst_row = pl.multiple_of(i * ROW_TILES, ROW_TILES)
        pltpu.make_async_copy(src_hbm.at[pl.ds(src_row, ROW_TILES), :],
                              dst.at[pl.ds(dst_row, ROW_TILES), :], sem).start()
        return carry
    lax.fori_loop(0, n_rows, body, 0, unroll=8)


def _tile_gather_start_inline(src_hbm, idx_ref, idx_off, n_rows, dst, sem):
    for i in range(n_rows):
        src_row = pl.multiple_of(idx_ref[0, idx_off + i] * ROW_TILES, ROW_TILES)
        pltpu.make_async_copy(src_hbm.at[pl.ds(src_row, ROW_TILES), :],
                              dst.at[pl.ds(i * ROW_TILES, ROW_TILES), :], sem).start(priority=i % 2)


RING = 3


def _expert_kernel(blk_ref, exp_ref, lo_ref, tok_cur, tok_nxt, tok_far, xf_hbm,
                   wg_ref, bg_ref, wu_ref, bu_ref, wd_ref, bd_ref, yb_ref, xbuf, sem, wgb, wub, wdb, *, n_items):
    w = pl.program_id(0)
    lo = lo_ref[w]
    slot = lax.rem(w, RING)
    far = lax.rem(w + RING - 1, RING)

    def wait_rows(s):
        pltpu.make_async_copy(xf_hbm.at[pl.ds(0, MOE_BLOCK * ROW_TILES), :], xbuf.at[s], sem.at[s]).wait()

    @pl.when(w == 0)
    def _():
        _tile_gather_start(xf_hbm, tok_cur, 0, MOE_BLOCK, xbuf.at[0], sem.at[0])
        _tile_gather_start(xf_hbm, tok_nxt, 0, MOE_BLOCK, xbuf.at[1], sem.at[1])

    @pl.when(jnp.logical_or(w == 0, exp_ref[w] != exp_ref[jnp.maximum(w - 1, 0)]))
    def _():
        wgb[...] = wg_ref[...].astype(BF16)
        wub[...] = wu_ref[...].astype(BF16)
        wdb[...] = wd_ref[...].astype(BF16)

    wait_rows(slot)
    xb = _load_token_tiles(xbuf.at[slot], MOE_BLOCK).astype(BF16)
    gate = jnp.dot(xb, wgb[...], preferred_element_type=F32) + bg_ref[...]
    up = jnp.dot(xb, wub[...], preferred_element_type=F32) + bu_ref[...]
    gate = jnp.minimum(gate, SWIGLU_LIMIT)
    up = jnp.clip(up, -SWIGLU_LIMIT, SWIGLU_LIMIT)
    glu = gate * jax.nn.sigmoid(gate * SWIGLU_ALPHA)
    h = ((up + 1.0) * glu).astype(BF16)
    y = jnp.dot(h, wdb[...], preferred_element_type=F32) + bd_ref[...]
    keep = lax.broadcasted_iota(jnp.int32, (MOE_BLOCK, D_MODEL), 0) < lo
    _store_token_tiles(yb_ref, jnp.where(keep, _load_token_tiles(yb_ref, MOE_BLOCK), y))
    _tile_gather_start_inline(xf_hbm, tok_far, 0, MOE_BLOCK, xbuf.at[far], sem.at[far])

    @pl.when(w == n_items - 1)
    def _():
        for s in range(1, RING):
            wait_rows(lax.rem(w + s, RING))


def _expert_call(slot_tok, items, xf, w_gate, b_gate, w_up, b_up, w_down, b_down):
    nblk = slot_tok.shape[0]
    n_items = items[0].shape[0]
    d, de = w_gate.shape[1], w_gate.shape[2]
    tok_spec = lambda f: pl.BlockSpec((None, 1, MOE_BLOCK), f, memory_space=pltpu.SMEM)
    w_spec = lambda r, c: pl.BlockSpec((None, r, c), lambda w, blk, exp, lo: (exp[w], 0, 0))
    grid_spec = pltpu.PrefetchScalarGridSpec(
        num_scalar_prefetch=3,
        grid=(n_items,),
        in_specs=[
            tok_spec(lambda w, blk, exp, lo: (blk[w], 0, 0)),
            tok_spec(lambda w, blk, exp, lo: (blk[jnp.minimum(w + 1, n_items - 1)], 0, 0)),
            tok_spec(lambda w, blk, exp, lo: (blk[jnp.minimum(w + RING - 1, n_items - 1)], 0, 0)),
            pl.BlockSpec(memory_space=pl.ANY),
            w_spec(d, de), w_spec(1, de), w_spec(d, de), w_spec(1, de), w_spec(de, d), w_spec(1, d),
        ],
        out_specs=pl.BlockSpec((MOE_BLOCK * ROW_TILES, LANES), lambda w, blk, exp, lo: (blk[w], 0)),
        scratch_shapes=[
            pltpu.VMEM((RING, MOE_BLOCK * ROW_TILES, LANES), F32),
            pltpu.SemaphoreType.DMA((RING,)),
            pltpu.VMEM((d, de), BF16), pltpu.VMEM((d, de), BF16), pltpu.VMEM((de, d), BF16),
        ],
    )
    return pl.pallas_call(
        functools.partial(_expert_kernel, n_items=n_items),
        grid_spec=grid_spec,
        out_shape=jax.ShapeDtypeStruct((nblk * MOE_BLOCK * ROW_TILES, LANES), F32),
        compiler_params=pltpu.CompilerParams(dimension_semantics=("arbitrary",), vmem_limit_bytes=VMEM_LIMIT),
        name="experts",
    )(*items, slot_tok, slot_tok, slot_tok, xf, w_gate, b_gate[:, None, :], w_up, b_up[:, None, :],
      w_down, b_down[:, None, :])


def _combine_kernel(pos_cur, pos_nxt, pos_far, x1_ref, rrow_ref, yb_hbm, fn_ref, out_ref, ybuf, sem, *, n):
    i = pl.program_id(0)
    slot = lax.rem(i, RING)
    far = lax.rem(i + RING - 1, RING)
    tt = x1_ref.shape[0]

    def wait_rows(s):
        for k in range(TOP_K):
            pltpu.make_async_copy(yb_hbm.at[pl.ds(0, tt * ROW_TILES), :], ybuf.at[s, k], sem.at[s]).wait()

    @pl.when(i == 0)
    def _():
        for k in range(TOP_K):
            _tile_gather_start(yb_hbm, pos_cur, k * tt, tt, ybuf.at[0, k], sem.at[0])
            _tile_gather_start(yb_hbm, pos_nxt, k * tt, tt, ybuf.at[1, k], sem.at[1])

    wait_rows(slot)
    rr = rrow_ref[...]
    acc = x1_ref[...]
    for k in range(TOP_K):
        acc = acc + rr[:, TOP_K + k:TOP_K + k + 1] * _load_token_tiles(ybuf.at[slot, k], tt)
    out_ref[...] = acc * lax.rsqrt(jnp.mean(acc * acc, axis=-1, keepdims=True) + EPS) * fn_ref[...]
    for k in range(TOP_K):
        _tile_gather_start_inline(yb_hbm, pos_far, k * tt, tt, ybuf.at[far, k], sem.at[far])

    @pl.when(i == n - 1)
    def _():
        for s in range(1, RING):
            wait_rows(lax.rem(i + s, RING))


def _combine_call(pos, x1, rrow, yb, fn, *, row_off, n_rows):
    tt = min(COMB_TT, n_rows)
    n_tiles = n_rows // tt
    off = row_off // tt
    d = x1.shape[1]
    pos = pos[:, row_off:row_off + n_rows].reshape(TOP_K, n_tiles, tt)
    pos = pos.transpose(1, 0, 2).reshape(n_tiles, 1, TOP_K * tt)
    pos_spec = lambda f: pl.BlockSpec((None, 1, tt * TOP_K), f, memory_space=pltpu.SMEM)
    return pl.pallas_call(
        functools.partial(_combine_kernel, n=n_tiles),
        grid=(n_tiles,),
        in_specs=[
            pos_spec(lambda i: (i, 0, 0)),
            pos_spec(lambda i: (jnp.minimum(i + 1, n_tiles - 1), 0, 0)),
            pos_spec(lambda i: (jnp.minimum(i + RING - 1, n_tiles - 1), 0, 0)),
            pl.BlockSpec((tt, d), lambda i: (off + i, 0)),
            pl.BlockSpec((tt, LANES), lambda i: (off + i, 0)),
            pl.BlockSpec(memory_space=pl.ANY),
            pl.BlockSpec((1, d), lambda i: (0, 0)),
        ],
        out_specs=pl.BlockSpec((tt, d), lambda i: (i, 0)),
        out_shape=jax.ShapeDtypeStruct((n_rows, d), F32),
        scratch_shapes=[pltpu.VMEM((RING, TOP_K, tt * ROW_TILES, LANES), F32), pltpu.SemaphoreType.DMA((RING,))],
        compiler_params=pltpu.CompilerParams(dimension_semantics=("arbitrary",), vmem_limit_bytes=VMEM_LIMIT),
        name="combine",
    )(pos, pos, pos, x1, rrow, yb, fn)


def kernel(x_prompt, x_sample, cache_conv_a, cache_conv_ssm, state_ssm, norm_mix, w_in, conv_a_w, norm_conv_a, conv_ssm_w, conv_ssm_b, dt_bias, a_log, d_skip, norm_ssm, w_out, norm_ffn, w_router, b_router, w_gate, b_gate, w_up, b_up, w_down, b_down, final_norm):
    bp, lp, _ = x_prompt.shape
    bs, ls, _ = x_sample.shape
    t_p, t_s = bp * lp, bs * ls
    t_all = t_p + t_s
    wts = _prep_mixer_weights(norm_mix, w_in, conv_a_w, norm_conv_a, conv_ssm_w, conv_ssm_b, dt_bias, a_log,
                              d_skip, norm_ssm, w_out, norm_ffn, w_router, b_router)
    zeros = lambda *s: jnp.zeros(s, F32)
    x1, xf, rrow, route_p, ca_p, cs_p, st_p, cnt = _mixer_call(
        x_prompt, zeros(bp, CONV_A_WIDTH - 1, D_CONV), zeros(bp, SSM_CONV_WIDTH - 1, D_XBC),
        zeros(bp, SSM_HEADS, SSM_HEAD_DIM, SSM_STATE), zeros(1, LANES), wts, _mixer_consts(MIX_Q),
        tl=MIX_TL, q=MIX_Q, t_all=t_all, row_off=0)
    x1, xf, rrow, route_s, ca_s, cs_s, st_s, cnt = _mixer_call(
        x_sample, cache_conv_a[0], cache_conv_ssm[0], state_ssm[0], cnt, wts,
        _mixer_consts(ls), tl=ls, q=ls, t_all=t_all, row_off=t_p, alias=(x1, xf, rrow))

    def rows(r0):
        return jnp.concatenate([route_p[:, r0:r0 + TOP_K, :].transpose(1, 0, 2).reshape(TOP_K, t_p),
                                route_s[:, r0:r0 + TOP_K, :].transpose(1, 0, 2).reshape(TOP_K, t_s)],
                               axis=1).astype(jnp.int32)
    slot_tok, pos, items = _routing(rows(0), rows(2 * TOP_K), cnt[0, :N_EXPERTS].astype(jnp.int32))

    yb = _expert_call(slot_tok, items, xf, w_gate[0], b_gate[0], w_up[0], b_up[0], w_down[0], b_down[0])
    fn = final_norm[None]
    y_p = _combine_call(pos, x1, rrow, yb, fn, row_off=0, n_rows=t_p)
    y_s = _combine_call(pos, x1, rrow, yb, fn, row_off=t_p, n_rows=t_s)
    return (y_p.reshape(bp, lp, D_MODEL), y_s.reshape(bs, ls, D_MODEL),
            ca_p[None], cs_p[None], st_p[None], ca_s[None], cs_s[None], st_s[None])
```

```python
import functools

import jax
import jax.numpy as jnp
from jax import lax
from jax.experimental import pallas as pl
from jax.experimental.pallas import tpu as pltpu

F32 = jnp.float32
BF16 = jnp.bfloat16

D_MODEL = 1024
D_CONV = 1024
D_SSM = 1024
CONV_GROUPS = 16
CONV_A_WIDTH = 3
SSM_HEAD_DIM = 64
SSM_HEADS = 16
SSM_GROUPS = 2
SSM_STATE = 128
SSM_CONV_WIDTH = 4
D_XBC = D_SSM + 2 * SSM_GROUPS * SSM_STATE
N_EXPERTS = 32
TOP_K = 4
SWIGLU_LIMIT = 7.0
SWIGLU_ALPHA = 1.702
MOE_BLOCK = 512
EPS = 1e-5

LANES = 128
COL_A = 3 * D_CONV
COL_S = D_SSM + D_XBC + LANES
D_IN_PAD = COL_A + COL_S
assert 2 * SSM_HEAD_DIM == LANES and SSM_STATE == LANES
NEG_BIG = -1e30
VMEM_LIMIT = 56 * 1024 * 1024


def _softplus(v):
    return jnp.maximum(v, 0.0) + jnp.log1p(jnp.exp(-jnp.abs(v)))


def _silu(v):
    return v * jax.nn.sigmoid(v)


def _split2(v):
    hi = v.astype(BF16)
    mid = (v - hi.astype(F32)).astype(BF16)
    return hi, mid


ROUTE_ROWS = 16
ROW_TILES = D_MODEL // LANES


def _store_token_tiles(ref, v, row0=0):
    r = v.shape[0]
    for s in range(ROW_TILES):
        ref[pl.ds(row0 * ROW_TILES + s, r, stride=ROW_TILES), :] = v[:, s * LANES:(s + 1) * LANES]


def _load_token_tiles(ref, r):
    return jnp.concatenate([ref[pl.ds(s, r, stride=ROW_TILES), :] for s in range(ROW_TILES)], axis=1)


def _transpose_rows(v):
    r = v.shape[0]
    if r % LANES:
        v = jnp.concatenate([v, jnp.zeros((LANES - r % LANES, v.shape[1]), v.dtype)], axis=0)
    return v.T[:, :r]


def _ssd_chunk(xs, bm, cm, dt, a_neg, z, dskip, s_scr, tril, e2):
    q = xs.shape[0]
    da = dt * a_neg
    d_hi = da.astype(BF16)
    r1 = da - d_hi.astype(F32)
    d_mid = r1.astype(BF16)
    d_lo = (r1 - d_mid.astype(F32)).astype(BF16)
    cs = jnp.dot(tril, jnp.concatenate([d_hi, d_mid, d_lo], axis=1), preferred_element_type=F32)
    acum = cs[:, 0:LANES] + cs[:, LANES:2 * LANES] + cs[:, 2 * LANES:3 * LANES]
    acum_t = _transpose_rows(acum)
    total = acum[q - 1:q, :]
    stack = jnp.concatenate(
        [dt, jnp.exp(total - acum), jnp.exp(acum), jnp.broadcast_to(jnp.exp(total), (8, LANES))], axis=0)
    s_hi, s_mid = _split2(stack)
    ex = jnp.dot(jnp.concatenate([s_hi, s_mid], axis=1), e2, preferred_element_type=F32)
    dt_e, dec_e, ea_e, cd_e = ex[0:q], ex[q:2 * q], ex[2 * q:3 * q], ex[3 * q:3 * q + 1]
    xdt = xs * dt_e
    xdt_b = xdt.astype(BF16)
    xdec_b = (xdt * dec_e).astype(BF16)
    row = lax.broadcasted_iota(jnp.int32, (q, q), 0)
    col = lax.broadcasted_iota(jnp.int32, (q, q), 1)
    causal = row >= col
    lane = lax.broadcasted_iota(jnp.int32, (q, LANES), 1)
    gw = D_SSM // SSM_GROUPS
    hpg = SSM_HEADS // SSM_GROUPS
    ys = []
    for g in range(SSM_GROUPS):
        cg = cm[:, g * SSM_STATE:(g + 1) * SSM_STATE].astype(BF16)
        bg = bm[:, g * SSM_STATE:(g + 1) * SSM_STATE].astype(BF16)
        cb = lax.dot_general(cg, bg, (((1,), (1,)), ((), ())), preferred_element_type=F32)
        s_g = s_scr[g]
        y_off = jnp.dot(cg, s_g.astype(BF16), preferred_element_type=F32) * ea_e[:, g * gw:(g + 1) * gw]
        pieces = []
        for j in range(hpg // 2):
            h0 = g * hpg + 2 * j
            ms = []
            for h in (h0, h0 + 1):
                diff = acum[:, h:h + 1] - acum_t[h:h + 1, :]
                ms.append(cb * jnp.exp(jnp.where(causal, diff, NEG_BIG)))
            m = jnp.concatenate(ms, axis=1).astype(BF16)
            xp = xdt_b[:, h0 * SSM_HEAD_DIM:(h0 + 2) * SSM_HEAD_DIM]
            zero = jnp.zeros_like(xp)
            rhs = jnp.concatenate([jnp.where(lane < SSM_HEAD_DIM, xp, zero),
                                   jnp.where(lane >= SSM_HEAD_DIM, xp, zero)], axis=0)
            pieces.append(jnp.dot(m, rhs, preferred_element_type=F32))
        ys.append(jnp.concatenate(pieces, axis=1) + y_off)
        upd = lax.dot_general(bg, xdec_b[:, g * gw:(g + 1) * gw], (((0,), (0,)), ((), ())),
                              preferred_element_type=F32)
        s_scr[g] = s_g * cd_e[:, g * gw:(g + 1) * gw] + upd
    y = jnp.concatenate(ys, axis=1) + dskip * xs
    return y * _silu(z)


def _mixer_kernel(*refs, tl, q, nl, n_alias):
    (x_ref, ca0_ref, cs0_ref, ssm0_ref, cnt0_ref, nmix_ref, win_ref, caw_ref, na_ref, csw_ref, csb_ref,
     dtb_ref, alog_ref, dsk_ref, ns_ref, wout_ref, nffn_ref, wr_ref, br_ref,
     tril_ref, e2_ref, g_ref, e3_ref) = refs[:23]
    refs = refs[23 + n_alias:]
    (x1_ref, xf_ref, rrow_ref, route_ref, ca_ref, cs_ref, ssm_ref, cnt_ref,
     upad, xpad, s_scr, ca_scr, cs_scr, cnt_scr) = refs
    l = pl.program_id(1)
    last = nl - 1
    pa_rows = CONV_A_WIDTH - 1
    ps_rows = SSM_CONV_WIDTH - 1

    @pl.when(jnp.logical_and(pl.program_id(0) == 0, l == 0))
    def _():
        cnt_scr[...] = cnt0_ref[...]

    @pl.when(l == 0)
    def _():
        ca_scr[...] = ca0_ref[...]
        cs_scr[...] = cs0_ref[...]
        for hp in range(SSM_HEADS // 2):
            g, j = divmod(hp, SSM_HEADS // SSM_GROUPS // 2)
            pair = jnp.concatenate([ssm0_ref[2 * hp], ssm0_ref[2 * hp + 1]], axis=0)
            s_scr[g, :, j * LANES:(j + 1) * LANES] = pair.T

    caw = caw_ref[...]
    csw = csw_ref[...]
    a_neg = -jnp.exp(alog_ref[...])
    gam = jnp.concatenate([na_ref[...], ns_ref[...]], axis=1)
    nb = D_SSM + SSM_GROUPS * SSM_STATE
    lane = lax.broadcasted_iota(jnp.int32, (q, LANES), 1)
    lane_f = lane.astype(F32)
    tail_a = ca_scr[...]
    tail_s = cs_scr[...]
    for j in range(tl // q):
        rows = slice(j * q, (j + 1) * q)
        x = x_ref[rows, :]
        xn = (x * lax.rsqrt(jnp.mean(x * x, axis=-1, keepdims=True) + EPS) * nmix_ref[...]).astype(BF16)

        pa = jnp.dot(xn, win_ref[:, 0:COL_A], preferred_element_type=F32)
        a_b, a_c, a_h = pa[:, 0:D_CONV], pa[:, D_CONV:2 * D_CONV], pa[:, 2 * D_CONV:3 * D_CONV]
        u = a_c * a_h
        upad[j, 8 - pa_rows:8, :] = tail_a
        upad[j, 8:8 + q, :] = u
        conv = upad[j, 8 - pa_rows:8 - pa_rows + q, :] * caw[0:1, :]
        for k in range(1, CONV_A_WIDTH):
            conv = conv + upad[j, 8 - pa_rows + k:8 - pa_rows + k + q, :] * caw[k:k + 1, :]
        y_a = a_b * conv
        tail_a = u[q - pa_rows:q, :]

        ps = jnp.dot(xn, win_ref[:, COL_A:D_IN_PAD], preferred_element_type=F32)
        z = ps[:, 0:D_SSM]
        xbc = ps[:, D_SSM:D_SSM + D_XBC]
        xpad[j, 8 - ps_rows:8, :] = tail_s
        xpad[j, 8:8 + q, :] = xbc
        dt = _softplus(ps[:, D_SSM + D_XBC:COL_S] + dtb_ref[...])
        acc = xpad[j, 8 - ps_rows:8 - ps_rows + q, :] * csw[0:1, :]
        for k in range(1, SSM_CONV_WIDTH):
            acc = acc + xpad[j, 8 - ps_rows + k:8 - ps_rows + k + q, :] * csw[k:k + 1, :]
        xc = _silu(acc + csb_ref[...])
        tail_s = xbc[q - ps_rows:q, :]
        y_s = _ssd_chunk(xc[:, 0:D_SSM], xc[:, D_SSM:nb], xc[:, nb:D_XBC], dt, a_neg, z, dsk_ref[...],
                         s_scr, tril_ref[...], e2_ref[...])

        mixed = jnp.concatenate([y_a, y_s], axis=1)
        mean = jnp.dot((mixed * mixed).astype(BF16), g_ref[...], preferred_element_type=F32)
        r_hi, r_mid = _split2(lax.rsqrt(mean + EPS))
        rs_e = jnp.dot(jnp.concatenate([r_hi, r_mid], axis=1), e3_ref[...], preferred_element_type=F32)
        mixed_n = (mixed * rs_e * gam).astype(BF16)
        x1 = x + jnp.dot(mixed_n, wout_ref[...], preferred_element_type=F32)
        x1_ref[rows, :] = x1

        xf = x1 * lax.rsqrt(jnp.mean(x1 * x1, axis=-1, keepdims=True) + EPS) * nffn_ref[...]
        _store_token_tiles(xf_ref, xf, row0=j * q)
        f_hi, f_mid = _split2(xf)
        logits = jnp.dot(jnp.concatenate([f_hi, f_hi, f_mid], axis=1), wr_ref[...],
                         preferred_element_type=F32) + br_ref[...]
        work = logits
        vals, idxs = [], []
        for _ in range(TOP_K):
            m = jnp.max(work, axis=-1, keepdims=True)
            idx = jnp.min(jnp.where(work == m, lane_f, float(LANES)), axis=-1, keepdims=True)
            vals.append(m)
            idxs.append(idx)
            work = jnp.where(lane_f == idx, -jnp.inf, work)
        es = [jnp.exp(v - vals[0]) for v in vals]
        inv = 1.0 / (es[0] + es[1] + es[2] + es[3])
        hots = [(lane_f == idxs[k]).astype(F32) for k in range(TOP_K)]
        hot = hots[0] + hots[1] + hots[2] + hots[3]
        before = jnp.dot(tril_ref[...], hot.astype(BF16), preferred_element_type=F32) - hot + cnt_scr[...]
        cnt_scr[...] = before[q - 1:q, :] + hot[q - 1:q, :]
        route = jnp.zeros((q, LANES), F32)
        for k in range(TOP_K):
            route = jnp.where(lane == k, idxs[k], route)
            route = jnp.where(lane == TOP_K + k, es[k] * inv, route)
            route = jnp.where(lane == 2 * TOP_K + k, jnp.sum(hots[k] * before, axis=-1, keepdims=True), route)
        rrow_ref[rows, :] = route
        route_ref[:, rows] = _transpose_rows(route)[0:ROUTE_ROWS, :]

    ca_scr[...] = tail_a
    cs_scr[...] = tail_s
    cnt_ref[...] = cnt_scr[...]

    @pl.when(l == last)
    def _():
        ca_ref[...] = tail_a
        cs_ref[...] = tail_s
        for hp in range(SSM_HEADS // 2):
            g, j = divmod(hp, SSM_HEADS // SSM_GROUPS // 2)
            pair = s_scr[g, :, j * LANES:(j + 1) * LANES].T
            ssm_ref[2 * hp] = pair[0:SSM_HEAD_DIM, :]
            ssm_ref[2 * hp + 1] = pair[SSM_HEAD_DIM:2 * SSM_HEAD_DIM, :]


def _const_spec(shape):
    nd = len(shape)
    return pl.BlockSpec(shape, lambda b, l: (0,) * nd, pipeline_mode=pl.Buffered(1))


def _mixer_call(x, ca0, cs0, ssm0, cnt0, wts, consts, *, tl, q, t_all, row_off, alias=None):
    b_sz, l_sz, _ = x.shape
    nl = l_sz // tl
    blk_off = row_off // tl
    n_alias = 0 if alias is None else len(alias)
    in_specs = [
        pl.BlockSpec((None, tl, D_MODEL), lambda b, l: (b, l, 0)),
        pl.BlockSpec((None, CONV_A_WIDTH - 1, D_CONV), lambda b, l: (b, 0, 0)),
        pl.BlockSpec((None, SSM_CONV_WIDTH - 1, D_XBC), lambda b, l: (b, 0, 0)),
        pl.BlockSpec((None, SSM_HEADS, SSM_HEAD_DIM, SSM_STATE), lambda b, l: (b, 0, 0, 0)),
        _const_spec(cnt0.shape),
    ] + [_const_spec(w.shape) for w in wts] + [_const_spec(c.shape) for c in consts]
    in_specs += [pl.BlockSpec(memory_space=pl.ANY)] * n_alias
    row_spec = pl.BlockSpec((tl, D_MODEL), lambda b, l: (blk_off + b * nl + l, 0))
    out_specs = [
        row_spec,
        pl.BlockSpec((tl * ROW_TILES, LANES), lambda b, l: (blk_off + b * nl + l, 0)),
        pl.BlockSpec((tl, LANES), lambda b, l: (blk_off + b * nl + l, 0)),
        pl.BlockSpec((None, ROUTE_ROWS, tl), lambda b, l: (b, 0, l)),
        pl.BlockSpec((None, CONV_A_WIDTH - 1, D_CONV), lambda b, l: (b, 0, 0)),
        pl.BlockSpec((None, SSM_CONV_WIDTH - 1, D_XBC), lambda b, l: (b, 0, 0)),
        pl.BlockSpec((None, SSM_HEADS, SSM_HEAD_DIM, SSM_STATE), lambda b, l: (b, 0, 0, 0)),
        pl.BlockSpec((1, LANES), lambda b, l: (0, 0)),
    ]
    out_shape = [
        jax.ShapeDtypeStruct((t_all, D_MODEL), F32),
        jax.ShapeDtypeStruct((t_all * ROW_TILES, LANES), F32),
        jax.ShapeDtypeStruct((t_all, LANES), F32),
        jax.ShapeDtypeStruct((b_sz, ROUTE_ROWS, l_sz), F32),
        jax.ShapeDtypeStruct((b_sz, CONV_A_WIDTH - 1, D_CONV), F32),
        jax.ShapeDtypeStruct((b_sz, SSM_CONV_WIDTH - 1, D_XBC), F32),
        jax.ShapeDtypeStruct((b_sz, SSM_HEADS, SSM_HEAD_DIM, SSM_STATE), F32),
        jax.ShapeDtypeStruct((1, LANES), F32),
    ]
    n_in = 5 + len(wts) + len(consts)
    aliases = {n_in + i: i for i in range(n_alias)}
    args = [x, ca0, cs0, ssm0, cnt0, *wts, *consts] + ([] if alias is None else list(alias))
    return pl.pallas_call(
        functools.partial(_mixer_kernel, tl=tl, q=q, nl=nl, n_alias=n_alias),
        grid=(b_sz, nl),
        in_specs=in_specs,
        out_specs=out_specs,
        out_shape=out_shape,
        scratch_shapes=[
            pltpu.VMEM((tl // q, q + 8, D_CONV), F32),
            pltpu.VMEM((tl // q, q + 8, D_XBC), F32),
            pltpu.VMEM((SSM_GROUPS, SSM_STATE, D_SSM // SSM_GROUPS), F32),
            pltpu.VMEM((CONV_A_WIDTH - 1, D_CONV), F32),
            pltpu.VMEM((SSM_CONV_WIDTH - 1, D_XBC), F32),
            pltpu.VMEM((1, LANES), F32),
        ],
        input_output_aliases=aliases,
        compiler_params=pltpu.CompilerParams(
            dimension_semantics=("arbitrary", "arbitrary"), vmem_limit_bytes=VMEM_LIMIT),
        name="mixer",
    )(*args)


def _mixer_consts(q):
    r = jnp.arange(q)
    tril = (r[:, None] >= r[None, :]).astype(BF16)
    j2 = jnp.arange(2 * LANES) % LANES
    e2 = (j2[:, None] == (jnp.arange(D_SSM) // SSM_HEAD_DIM)[None, :]).astype(BF16)
    c = jnp.arange(D_CONV + D_SSM)
    grp = jnp.where(c < D_CONV, c // (D_CONV // CONV_GROUPS),
                    CONV_GROUPS + (c - D_CONV) // (D_SSM // SSM_GROUPS))
    wgt = jnp.where(c < D_CONV, CONV_GROUPS / D_CONV, SSM_GROUPS / D_SSM)
    g = jnp.where(grp[:, None] == jnp.arange(LANES)[None, :], wgt[:, None], 0.0).astype(BF16)
    e3 = (j2[:, None] == grp[None, :]).astype(BF16)
    return tril, e2, g, e3


def _prep_mixer_weights(norm_mix, w_in, conv_a_w, norm_conv_a, conv_ssm_w, conv_ssm_b, dt_bias, a_log,
                        d_skip, norm_ssm, w_out, norm_ffn, w_router, b_router):
    hpad = LANES - SSM_HEADS
    w_in_p = jnp.pad(w_in[0], ((0, 0), (0, hpad))).astype(BF16)
    wr = jnp.pad(w_router[0], ((0, 0), (0, LANES - N_EXPERTS)))
    wr_hi = wr.astype(BF16)
    wr_mid = (wr - wr_hi.astype(F32)).astype(BF16)
    wr3 = jnp.concatenate([wr_hi, wr_mid, wr_hi], axis=0)
    br = jnp.pad(b_router[0], (0, LANES - N_EXPERTS), constant_values=NEG_BIG)[None]
    return (norm_mix[0][None], w_in_p, conv_a_w[0], norm_conv_a[0][None], conv_ssm_w[0],
            conv_ssm_b[0][None], jnp.pad(dt_bias[0], (0, hpad))[None], jnp.pad(a_log[0], (0, hpad))[None],
            jnp.repeat(d_skip[0], SSM_HEAD_DIM)[None], norm_ssm[0][None], w_out[0].astype(BF16),
            norm_ffn[0][None], wr3, br)


MIX_TL = 512
MIX_Q = 256
COMB_TT = 256


def _routing(top_i, rank, counts):
    t = top_i.shape[1]
    n = t * TOP_K
    assert n % MOE_BLOCK == 0
    nblk = n // MOE_BLOCK
    n_items = nblk + N_EXPERTS
    shift = max(n - 1, 1).bit_length()
    assert N_EXPERTS << shift < 2 ** 31
    ent = (jnp.arange(t, dtype=jnp.int32) * TOP_K)[None, :] + jnp.arange(TOP_K, dtype=jnp.int32)[:, None]
    key = lax.sort((top_i * (1 << shift) + ent).reshape(n))
    slot_tok = (key & ((1 << shift) - 1)) // TOP_K
    group_end = jnp.cumsum(counts)
    group_start = group_end - counts
    pos = rank
    for e in range(N_EXPERTS):
        pos = pos + jnp.where(top_i == e, group_start[e], 0)
    bp = lax.sort(jnp.concatenate([jnp.arange(nblk, dtype=jnp.int32) * MOE_BLOCK, group_start]))
    dup = jnp.concatenate([jnp.zeros((1,), bool), bp[1:] == bp[:-1]])
    bp = lax.sort(jnp.where(jnp.logical_or(dup, bp >= n), n, bp))
    blk = jnp.minimum(bp // MOE_BLOCK, nblk - 1)
    lo = bp - blk * MOE_BLOCK
    exp = jnp.sum((group_end[None, :] <= jnp.minimum(bp, n - 1)[:, None]).astype(jnp.int32), axis=1)
    items = tuple(a.astype(jnp.int32) for a in (blk, exp, lo))
    assert items[0].shape == (n_items,)
    return slot_tok.astype(jnp.int32).reshape(nblk, 1, MOE_BLOCK), pos.astype(jnp.int32), items


def _tile_gather_start(src_hbm, idx_ref, idx_off, n_rows, dst, sem):
    def body(i, carry):
        src_row = pl.multiple_of(idx_ref[0, idx_off + i] * ROW_TILES, ROW_TILES)
        dst_row = pl.multiple_of(i * ROW_TILES, ROW_TILES)
        pltpu.make_async_copy(src_hbm.at[pl.ds(src_row, ROW_TILES), :],
                              dst.at[pl.ds(dst_row, ROW_TILES), :], sem).start()
        return carry
    lax.fori_loop(0, n_rows, body, 0, unroll=8)


def _tile_gather_start_inline(src_hbm, idx_ref, idx_off, n_rows, dst, sem):
    for i in range(n_rows):
        src_row = pl.multiple_of(idx_ref[0, idx_off + i] * ROW_TILES, ROW_TILES)
        pltpu.make_async_copy(src_hbm.at[pl.ds(src_row, ROW_TILES), :],
                              dst.at[pl.ds(i * ROW_TILES, ROW_TILES), :], sem).start(priority=i % 2)


RING = 3


def _expert_kernel(blk_ref, exp_ref, lo_ref, tok_cur, tok_nxt, tok_far, xf_hbm,
                   wg_ref, bg_ref, wu_ref, bu_ref, wd_ref, bd_ref, yb_ref, xbuf, sem, wgb, wub, wdb, *, n_items):
    w = pl.program_id(0)
    lo = lo_ref[w]
    slot = lax.rem(w, RING)
    far = lax.rem(w + RING - 1, RING)

    def wait_rows(s):
        pltpu.make_async_copy(xf_hbm.at[pl.ds(0, MOE_BLOCK * ROW_TILES), :], xbuf.at[s], sem.at[s]).wait()

    @pl.when(w == 0)
    def _():
        _tile_gather_start(xf_hbm, tok_cur, 0, MOE_BLOCK, xbuf.at[0], sem.at[0])
        _tile_gather_start(xf_hbm, tok_nxt, 0, MOE_BLOCK, xbuf.at[1], sem.at[1])

    @pl.when(jnp.logical_or(w == 0, exp_ref[w] != exp_ref[jnp.maximum(w - 1, 0)]))
    def _():
        wgb[...] = wg_ref[...].astype(BF16)
        wub[...] = wu_ref[...].astype(BF16)
        wdb[...] = wd_ref[...].astype(BF16)

    wait_rows(slot)
    xb = _load_token_tiles(xbuf.at[slot], MOE_BLOCK).astype(BF16)
    gate = jnp.dot(xb, wgb[...], preferred_element_type=F32) + bg_ref[...]
    up = jnp.dot(xb, wub[...], preferred_element_type=F32) + bu_ref[...]
    gate = jnp.minimum(gate, SWIGLU_LIMIT)
    up = jnp.clip(up, -SWIGLU_LIMIT, SWIGLU_LIMIT)
    glu = gate * jax.nn.sigmoid(gate * SWIGLU_ALPHA)
    h = ((up + 1.0) * glu).astype(BF16)
    y = jnp.dot(h, wdb[...], preferred_element_type=F32) + bd_ref[...]
    keep = lax.broadcasted_iota(jnp.int32, (MOE_BLOCK, D_MODEL), 0) < lo
    _store_token_tiles(yb_ref, jnp.where(keep, _load_token_tiles(yb_ref, MOE_BLOCK), y))
    _tile_gather_start_inline(xf_hbm, tok_far, 0, MOE_BLOCK, xbuf.at[far], sem.at[far])

    @pl.when(w == n_items - 1)
    def _():
        for s in range(1, RING):
            wait_rows(lax.rem(w + s, RING))


def _expert_call(slot_tok, items, xf, w_gate, b_gate, w_up, b_up, w_down, b_down):
    nblk = slot_tok.shape[0]
    n_items = items[0].shape[0]
    d, de = w_gate.shape[1], w_gate.shape[2]
    tok_spec = lambda f: pl.BlockSpec((None, 1, MOE_BLOCK), f, memory_space=pltpu.SMEM)
    w_spec = lambda r, c: pl.BlockSpec((None, r, c), lambda w, blk, exp, lo: (exp[w], 0, 0))
    grid_spec = pltpu.PrefetchScalarGridSpec(
        num_scalar_prefetch=3,
        grid=(n_items,),
        in_specs=[
            tok_spec(lambda w, blk, exp, lo: (blk[w], 0, 0)),
            tok_spec(lambda w, blk, exp, lo: (blk[jnp.minimum(w + 1, n_items - 1)], 0, 0)),
            tok_spec(lambda w, blk, exp, lo: (blk[jnp.minimum(w + RING - 1, n_items - 1)], 0, 0)),
            pl.BlockSpec(memory_space=pl.ANY),
            w_spec(d, de), w_spec(1, de), w_spec(d, de), w_spec(1, de), w_spec(de, d), w_spec(1, d),
        ],
        out_specs=pl.BlockSpec((MOE_BLOCK * ROW_TILES, LANES), lambda w, blk, exp, lo: (blk[w], 0)),
        scratch_shapes=[
            pltpu.VMEM((RING, MOE_BLOCK * ROW_TILES, LANES), F32),
            pltpu.SemaphoreType.DMA((RING,)),
            pltpu.VMEM((d, de), BF16), pltpu.VMEM((d, de), BF16), pltpu.VMEM((de, d), BF16),
        ],
    )
    return pl.pallas_call(
        functools.partial(_expert_kernel, n_items=n_items),
        grid_spec=grid_spec,
        out_shape=jax.ShapeDtypeStruct((nblk * MOE_BLOCK * ROW_TILES, LANES), F32),
        compiler_params=pltpu.CompilerParams(dimension_semantics=("arbitrary",), vmem_limit_bytes=VMEM_LIMIT),
        name="experts",
    )(*items, slot_tok, slot_tok, slot_tok, xf, w_gate, b_gate[:, None, :], w_up, b_up[:, None, :],
      w_down, b_down[:, None, :])


def _combine_kernel(pos_cur, pos_nxt, pos_far, x1_ref, rrow_ref, yb_hbm, fn_ref, out_ref, ybuf, sem, *, n):
    i = pl.program_id(0)
    slot = lax.rem(i, RING)
    far = lax.rem(i + RING - 1, RING)
    tt = x1_ref.shape[0]

    def wait_rows(s):
        for k in range(TOP_K):
            pltpu.make_async_copy(yb_hbm.at[pl.ds(0, tt * ROW_TILES), :], ybuf.at[s, k], sem.at[s]).wait()

    @pl.when(i == 0)
    def _():
        for k in range(TOP_K):
            _tile_gather_start(yb_hbm, pos_cur, k * tt, tt, ybuf.at[0, k], sem.at[0])
            _tile_gather_start(yb_hbm, pos_nxt, k * tt, tt, ybuf.at[1, k], sem.at[1])

    wait_rows(slot)
    rr = rrow_ref[...]
    acc = x1_ref[...]
    for k in range(TOP_K):
        acc = acc + rr[:, TOP_K + k:TOP_K + k + 1] * _load_token_tiles(ybuf.at[slot, k], tt)
    out_ref[...] = acc * lax.rsqrt(jnp.mean(acc * acc, axis=-1, keepdims=True) + EPS) * fn_ref[...]
    for k in range(TOP_K):
        _tile_gather_start_inline(yb_hbm, pos_far, k * tt, tt, ybuf.at[far, k], sem.at[far])

    @pl.when(i == n - 1)
    def _():
        for s in range(1, RING):
            wait_rows(lax.rem(i + s, RING))


def _combine_call(pos, x1, rrow, yb, fn, *, row_off, n_rows):
    tt = min(COMB_TT, n_rows)
    n_tiles = n_rows // tt
    off = row_off // tt
    d = x1.shape[1]
    pos = pos[:, row_off:row_off + n_rows].reshape(TOP_K, n_tiles, tt)
    pos = pos.transpose(1, 0, 2).reshape(n_tiles, 1, TOP_K * tt)
    pos_spec = lambda f: pl.BlockSpec((None, 1, tt * TOP_K), f, memory_space=pltpu.SMEM)
    return pl.pallas_call(
        functools.partial(_combine_kernel, n=n_tiles),
        grid=(n_tiles,),
        in_specs=[
            pos_spec(lambda i: (i, 0, 0)),
            pos_spec(lambda i: (jnp.minimum(i + 1, n_tiles - 1), 0, 0)),
            pos_spec(lambda i: (jnp.minimum(i + RING - 1, n_tiles - 1), 0, 0)),
            pl.BlockSpec((tt, d), lambda i: (off + i, 0)),
            pl.BlockSpec((tt, LANES), lambda i: (off + i, 0)),
            pl.BlockSpec(memory_space=pl.ANY),
            pl.BlockSpec((1, d), lambda i: (0, 0)),
        ],
        out_specs=pl.BlockSpec((tt, d), lambda i: (i, 0)),
        out_shape=jax.ShapeDtypeStruct((n_rows, d), F32),
        scratch_shapes=[pltpu.VMEM((RING, TOP_K, tt * ROW_TILES, LANES), F32), pltpu.SemaphoreType.DMA((RING,))],
        compiler_params=pltpu.CompilerParams(dimension_semantics=("arbitrary",), vmem_limit_bytes=VMEM_LIMIT),
        name="combine",
    )(pos, pos, pos, x1, rrow, yb, fn)


def kernel(x_prompt, x_sample, cache_conv_a, cache_conv_ssm, state_ssm, norm_mix, w_in, conv_a_w, norm_conv_a, conv_ssm_w, conv_ssm_b, dt_bias, a_log, d_skip, norm_ssm, w_out, norm_ffn, w_router, b_router, w_gate, b_gate, w_up, b_up, w_down, b_down, final_norm):
    bp, lp, _ = x_prompt.shape
    bs, ls, _ = x_sample.shape
    t_p, t_s = bp * lp, bs * ls
    t_all = t_p + t_s
    wts = _prep_mixer_weights(norm_mix, w_in, conv_a_w, norm_conv_a, conv_ssm_w, conv_ssm_b, dt_bias, a_log,
                              d_skip, norm_ssm, w_out, norm_ffn, w_router, b_router)
    zeros = lambda *s: jnp.zeros(s, F32)
    x1, xf, rrow, route_p, ca_p, cs_p, st_p, cnt = _mixer_call(
        x_prompt, zeros(bp, CONV_A_WIDTH - 1, D_CONV), zeros(bp, SSM_CONV_WIDTH - 1, D_XBC),
        zeros(bp, SSM_HEADS, SSM_HEAD_DIM, SSM_STATE), zeros(1, LANES), wts, _mixer_consts(MIX_Q),
        tl=MIX_TL, q=MIX_Q, t_all=t_all, row_off=0)
    x1, xf, rrow, route_s, ca_s, cs_s, st_s, cnt = _mixer_call(
        x_sample, cache_conv_a[0], cache_conv_ssm[0], state_ssm[0], cnt, wts,
        _mixer_consts(ls), tl=ls, q=ls, t_all=t_all, row_off=t_p, alias=(x1, xf, rrow))

    def rows(r0):
        return jnp.concatenate([route_p[:, r0:r0 + TOP_K, :].transpose(1, 0, 2).reshape(TOP_K, t_p),
                                route_s[:, r0:r0 + TOP_K, :].transpose(1, 0, 2).reshape(TOP_K, t_s)],
                               axis=1).astype(jnp.int32)
    slot_tok, pos, items = _routing(rows(0), rows(2 * TOP_K), cnt[0, :N_EXPERTS].astype(jnp.int32))

    yb = _expert_call(slot_tok, items, xf, w_gate[0], b_gate[0], w_up[0], b_up[0], w_down[0], b_down[0])
    fn = final_norm[None]
    y_p = _combine_call(pos, x1, rrow, yb, fn, row_off=0, n_rows=t_p)
    y_s = _combine_call(pos, x1, rrow, yb, fn, row_off=t_p, n_rows=t_s)
    return (y_p.reshape(bp, lp, D_MODEL), y_s.reshape(bs, ls, D_MODEL),
            ca_p[None], cs_p[None], st_p[None], ca_s[None], cs_s[None], st_s[None])
```

```python
import functools

import jax
import jax.numpy as jnp
from jax import lax
from jax.experimental import pallas as pl
from jax.experimental.pallas import tpu as pltpu

F32 = jnp.float32
BF16 = jnp.bfloat16

D_MODEL = 1024
D_CONV = 1024
D_SSM = 1024
CONV_GROUPS = 16
CONV_A_WIDTH = 3
SSM_HEAD_DIM = 64
SSM_HEADS = 16
SSM_GROUPS = 2
SSM_STATE = 128
SSM_CONV_WIDTH = 4
D_XBC = D_SSM + 2 * SSM_GROUPS * SSM_STATE
N_EXPERTS = 32
TOP_K = 4
SWIGLU_LIMIT = 7.0
SWIGLU_ALPHA = 1.702
MOE_BLOCK = 512
EPS = 1e-5

LANES = 128
COL_A = 3 * D_CONV
COL_S = D_SSM + D_XBC + LANES
D_IN_PAD = COL_A + COL_S
assert 2 * SSM_HEAD_DIM == LANES and SSM_STATE == LANES
NEG_BIG = -1e30
VMEM_LIMIT = 56 * 1024 * 1024


def _softplus(v):
    return jnp.maximum(v, 0.0) + jnp.log1p(jnp.exp(-jnp.abs(v)))


def _silu(v):
    return v * jax.nn.sigmoid(v)


def _split2(v):
    hi = v.astype(BF16)
    mid = (v - hi.astype(F32)).astype(BF16)
    return hi, mid


ROUTE_ROWS = 16
ROW_TILES = D_MODEL // LANES


def _store_token_tiles(ref, v, row0=0):
    r = v.shape[0]
    for s in range(ROW_TILES):
        ref[pl.ds(row0 * ROW_TILES + s, r, stride=ROW_TILES), :] = v[:, s * LANES:(s + 1) * LANES]


def _load_token_tiles(ref, r):
    return jnp.concatenate([ref[pl.ds(s, r, stride=ROW_TILES), :] for s in range(ROW_TILES)], axis=1)


def _transpose_rows(v):
    r = v.shape[0]
    if r % LANES:
        v = jnp.concatenate([v, jnp.zeros((LANES - r % LANES, v.shape[1]), v.dtype)], axis=0)
    return v.T[:, :r]


def _ssd_chunk(xs, bm, cm, dt, a_neg, z, dskip, s_scr, tril, e2):
    q = xs.shape[0]
    da = dt * a_neg
    d_hi = da.astype(BF16)
    r1 = da - d_hi.astype(F32)
    d_mid = r1.astype(BF16)
    d_lo = (r1 - d_mid.astype(F32)).astype(BF16)
    cs = jnp.dot(tril, jnp.concatenate([d_hi, d_mid, d_lo], axis=1), preferred_element_type=F32)
    acum = cs[:, 0:LANES] + cs[:, LANES:2 * LANES] + cs[:, 2 * LANES:3 * LANES]
    acum_t = _transpose_rows(acum)
    total = acum[q - 1:q, :]
    stack = jnp.concatenate(
        [dt, jnp.exp(total - acum), jnp.exp(acum), jnp.broadcast_to(jnp.exp(total), (8, LANES))], axis=0)
    s_hi, s_mid = _split2(stack)
    ex = jnp.dot(jnp.concatenate([s_hi, s_mid], axis=1), e2, preferred_element_type=F32)
    dt_e, dec_e, ea_e, cd_e = ex[0:q], ex[q:2 * q], ex[2 * q:3 * q], ex[3 * q:3 * q + 1]
    xdt = xs * dt_e
    xdt_b = xdt.astype(BF16)
    xdec_b = (xdt * dec_e).astype(BF16)
    row = lax.broadcasted_iota(jnp.int32, (q, q), 0)
    col = lax.broadcasted_iota(jnp.int32, (q, q), 1)
    causal = row >= col
    lane = lax.broadcasted_iota(jnp.int32, (q, LANES), 1)
    gw = D_SSM // SSM_GROUPS
    hpg = SSM_HEADS // SSM_GROUPS
    ys = []
    for g in range(SSM_GROUPS):
        cg = cm[:, g * SSM_STATE:(g + 1) * SSM_STATE].astype(BF16)
        bg = bm[:, g * SSM_STATE:(g + 1) * SSM_STATE].astype(BF16)
        cb = lax.dot_general(cg, bg, (((1,), (1,)), ((), ())), preferred_element_type=F32)
        s_g = s_scr[g]
        y_off = jnp.dot(cg, s_g.astype(BF16), preferred_element_type=F32) * ea_e[:, g * gw:(g + 1) * gw]
        pieces = []
        for j in range(hpg // 2):
            h0 = g * hpg + 2 * j
            ms = []
            for h in (h0, h0 + 1):
                diff = acum[:, h:h + 1] - acum_t[h:h + 1, :]
                ms.append(cb * jnp.exp(jnp.where(causal, diff, NEG_BIG)))
            m = jnp.concatenate(ms, axis=1).astype(BF16)
            xp = xdt_b[:, h0 * SSM_HEAD_DIM:(h0 + 2) * SSM_HEAD_DIM]
            zero = jnp.zeros_like(xp)
            rhs = jnp.concatenate([jnp.where(lane < SSM_HEAD_DIM, xp, zero),
                                   jnp.where(lane >= SSM_HEAD_DIM, xp, zero)], axis=0)
            pieces.append(jnp.dot(m, rhs, preferred_element_type=F32))
        ys.append(jnp.concatenate(pieces, axis=1) + y_off)
        upd = lax.dot_general(bg, xdec_b[:, g * gw:(g + 1) * gw], (((0,), (0,)), ((), ())),
                              preferred_element_type=F32)
        s_scr[g] = s_g * cd_e[:, g * gw:(g + 1) * gw] + upd
    y = jnp.concatenate(ys, axis=1) + dskip * xs
    return y * _silu(z)


def _mixer_kernel(*refs, tl, q, nl, n_alias):
    (x_ref, ca0_ref, cs0_ref, ssm0_ref, cnt0_ref, nmix_ref, win_ref, caw_ref, na_ref, csw_ref, csb_ref,
     dtb_ref, alog_ref, dsk_ref, ns_ref, wout_ref, nffn_ref, wr_ref, br_ref,
     tril_ref, e2_ref, g_ref, e3_ref) = refs[:23]
    refs = refs[23 + n_alias:]
    (x1_ref, xf_ref, rrow_ref, route_ref, ca_ref, cs_ref, ssm_ref, cnt_ref,
     upad, xpad, s_scr, ca_scr, cs_scr, cnt_scr) = refs
    l = pl.program_id(1)
    last = nl - 1
    pa_rows = CONV_A_WIDTH - 1
    ps_rows = SSM_CONV_WIDTH - 1

    @pl.when(jnp.logical_and(pl.program_id(0) == 0, l == 0))
    def _():
        cnt_scr[...] = cnt0_ref[...]

    @pl.when(l == 0)
    def _():
        ca_scr[...] = ca0_ref[...]
        cs_scr[...] = cs0_ref[...]
        for hp in range(SSM_HEADS // 2):
            g, j = divmod(hp, SSM_HEADS // SSM_GROUPS // 2)
            pair = jnp.concatenate([ssm0_ref[2 * hp], ssm0_ref[2 * hp + 1]], axis=0)
            s_scr[g, :, j * LANES:(j + 1) * LANES] = pair.T

    caw = caw_ref[...]
    csw = csw_ref[...]
    a_neg = -jnp.exp(alog_ref[...])
    gam = jnp.concatenate([na_ref[...], ns_ref[...]], axis=1)
    nb = D_SSM + SSM_GROUPS * SSM_STATE
    lane = lax.broadcasted_iota(jnp.int32, (q, LANES), 1)
    lane_f = lane.astype(F32)
    tail_a = ca_scr[...]
    tail_s = cs_scr[...]
    for j in range(tl // q):
        rows = slice(j * q, (j + 1) * q)
        x = x_ref[rows, :]
        xn = (x * lax.rsqrt(jnp.mean(x * x, axis=-1, keepdims=True) + EPS) * nmix_ref[...]).astype(BF16)

        pa = jnp.dot(xn, win_ref[:, 0:COL_A], preferred_element_type=F32)
        a_b, a_c, a_h = pa[:, 0:D_CONV], pa[:, D_CONV:2 * D_CONV], pa[:, 2 * D_CONV:3 * D_CONV]
        u = a_c * a_h
        upad[j, 8 - pa_rows:8, :] = tail_a
        upad[j, 8:8 + q, :] = u
        conv = upad[j, 8 - pa_rows:8 - pa_rows + q, :] * caw[0:1, :]
        for k in range(1, CONV_A_WIDTH):
            conv = conv + upad[j, 8 - pa_rows + k:8 - pa_rows + k + q, :] * caw[k:k + 1, :]
        y_a = a_b * conv
        tail_a = u[q - pa_rows:q, :]

        ps = jnp.dot(xn, win_ref[:, COL_A:D_IN_PAD], preferred_element_type=F32)
        z = ps[:, 0:D_SSM]
        xbc = ps[:, D_SSM:D_SSM + D_XBC]
        xpad[j, 8 - ps_rows:8, :] = tail_s
        xpad[j, 8:8 + q, :] = xbc
        dt = _softplus(ps[:, D_SSM + D_XBC:COL_S] + dtb_ref[...])
        acc = xpad[j, 8 - ps_rows:8 - ps_rows + q, :] * csw[0:1, :]
        for k in range(1, SSM_CONV_WIDTH):
            acc = acc + xpad[j, 8 - ps_rows + k:8 - ps_rows + k + q, :] * csw[k:k + 1, :]
        xc = _silu(acc + csb_ref[...])
        tail_s = xbc[q - ps_rows:q, :]
        y_s = _ssd_chunk(xc[:, 0:D_SSM], xc[:, D_SSM:nb], xc[:, nb:D_XBC], dt, a_neg, z, dsk_ref[...],
                         s_scr, tril_ref[...], e2_ref[...])

        mixed = jnp.concatenate([y_a, y_s], axis=1)
        mean = jnp.dot((mixed * mixed).astype(BF16), g_ref[...], preferred_element_type=F32)
        r_hi, r_mid = _split2(lax.rsqrt(mean + EPS))
        rs_e = jnp.dot(jnp.concatenate([r_hi, r_mid], axis=1), e3_ref[...], preferred_element_type=F32)
        mixed_n = (mixed * rs_e * gam).astype(BF16)
        x1 = x + jnp.dot(mixed_n, wout_ref[...], preferred_element_type=F32)
        x1_ref[rows, :] = x1

        xf = x1 * lax.rsqrt(jnp.mean(x1 * x1, axis=-1, keepdims=True) + EPS) * nffn_ref[...]
        _store_token_tiles(xf_ref, xf, row0=j * q)
        f_hi, f_mid = _split2(xf)
        logits = jnp.dot(jnp.concatenate([f_hi, f_hi, f_mid], axis=1), wr_ref[...],
                         preferred_element_type=F32) + br_ref[...]
        work = logits
        vals, idxs = [], []
        for _ in range(TOP_K):
            m = jnp.max(work, axis=-1, keepdims=True)
            idx = jnp.min(jnp.where(work == m, lane_f, float(LANES)), axis=-1, keepdims=True)
            vals.append(m)
            idxs.append(idx)
            work = jnp.where(lane_f == idx, -jnp.inf, work)
        es = [jnp.exp(v - vals[0]) for v in vals]
        inv = 1.0 / (es[0] + es[1] + es[2] + es[3])
        hots = [(lane_f == idxs[k]).astype(F32) for k in range(TOP_K)]
        hot = hots[0] + hots[1] + hots[2] + hots[3]
        before = jnp.dot(tril_ref[...], hot.astype(BF16), preferred_element_type=F32) - hot + cnt_scr[...]
        cnt_scr[...] = before[q - 1:q, :] + hot[q - 1:q, :]
        route = jnp.zeros((q, LANES), F32)
        for k in range(TOP_K):
            route = jnp.where(lane == k, idxs[k], route)
            route = jnp.where(lane == TOP_K + k, es[k] * inv, route)
            route = jnp.where(lane == 2 * TOP_K + k, jnp.sum(hots[k] * before, axis=-1, keepdims=True), route)
        rrow_ref[rows, :] = route
        route_ref[:, rows] = _transpose_rows(route)[0:ROUTE_ROWS, :]

    ca_scr[...] = tail_a
    cs_scr[...] = tail_s
    cnt_ref[...] = cnt_scr[...]

    @pl.when(l == last)
    def _():
        ca_ref[...] = tail_a
        cs_ref[...] = tail_s
        for hp in range(SSM_HEADS // 2):
            g, j = divmod(hp, SSM_HEADS // SSM_GROUPS // 2)
            pair = s_scr[g, :, j * LANES:(j + 1) * LANES].T
            ssm_ref[2 * hp] = pair[0:SSM_HEAD_DIM, :]
            ssm_ref[2 * hp + 1] = pair[SSM_HEAD_DIM:2 * SSM_HEAD_DIM, :]


def _const_spec(shape):
    nd = len(shape)
    return pl.BlockSpec(shape, lambda b, l: (0,) * nd, pipeline_mode=pl.Buffered(1))


def _mixer_call(x, ca0, cs0, ssm0, cnt0, wts, consts, *, tl, q, t_all, row_off, alias=None):
    b_sz, l_sz, _ = x.shape
    nl = l_sz // tl
    blk_off = row_off // tl
    n_alias = 0 if alias is None else len(alias)
    in_specs = [
        pl.BlockSpec((None, tl, D_MODEL), lambda b, l: (b, l, 0)),
        pl.BlockSpec((None, CONV_A_WIDTH - 1, D_CONV), lambda b, l: (b, 0, 0)),
        pl.BlockSpec((None, SSM_CONV_WIDTH - 1, D_XBC), lambda b, l: (b, 0, 0)),
        pl.BlockSpec((None, SSM_HEADS, SSM_HEAD_DIM, SSM_STATE), lambda b, l: (b, 0, 0, 0)),
        _const_spec(cnt0.shape),
    ] + [_const_spec(w.shape) for w in wts] + [_const_spec(c.shape) for c in consts]
    in_specs += [pl.BlockSpec(memory_space=pl.ANY)] * n_alias
    row_spec = pl.BlockSpec((tl, D_MODEL), lambda b, l: (blk_off + b * nl + l, 0))
    out_specs = [
        row_spec,
        pl.BlockSpec((tl * ROW_TILES, LANES), lambda b, l: (blk_off + b * nl + l, 0)),
        pl.BlockSpec((tl, LANES), lambda b, l: (blk_off + b * nl + l, 0)),
        pl.BlockSpec((None, ROUTE_ROWS, tl), lambda b, l: (b, 0, l)),
        pl.BlockSpec((None, CONV_A_WIDTH - 1, D_CONV), lambda b, l: (b, 0, 0)),
        pl.BlockSpec((None, SSM_CONV_WIDTH - 1, D_XBC), lambda b, l: (b, 0, 0)),
        pl.BlockSpec((None, SSM_HEADS, SSM_HEAD_DIM, SSM_STATE), lambda b, l: (b, 0, 0, 0)),
        pl.BlockSpec((1, LANES), lambda b, l: (0, 0)),
    ]
    out_shape = [
        jax.ShapeDtypeStruct((t_all, D_MODEL), F32),
        jax.ShapeDtypeStruct((t_all * ROW_TILES, LANES), F32),
        jax.ShapeDtypeStruct((t_all, LANES), F32),
        jax.ShapeDtypeStruct((b_sz, ROUTE_ROWS, l_sz), F32),
        jax.ShapeDtypeStruct((b_sz, CONV_A_WIDTH - 1, D_CONV), F32),
        jax.ShapeDtypeStruct((b_sz, SSM_CONV_WIDTH - 1, D_XBC), F32),
        jax.ShapeDtypeStruct((b_sz, SSM_HEADS, SSM_HEAD_DIM, SSM_STATE), F32),
        jax.ShapeDtypeStruct((1, LANES), F32),
    ]
    n_in = 5 + len(wts) + len(consts)
    aliases = {n_in + i: i for i in range(n_alias)}
    args = [x, ca0, cs0, ssm0, cnt0, *wts, *consts] + ([] if alias is None else list(alias))
    return pl.pallas_call(
        functools.partial(_mixer_kernel, tl=tl, q=q, nl=nl, n_alias=n_alias),
        grid=(b_sz, nl),
        in_specs=in_specs,
        out_specs=out_specs,
        out_shape=out_shape,
        scratch_shapes=[
            pltpu.VMEM((tl // q, q + 8, D_CONV), F32),
            pltpu.VMEM((tl // q, q + 8, D_XBC), F32),
            pltpu.VMEM((SSM_GROUPS, SSM_STATE, D_SSM // SSM_GROUPS), F32),
            pltpu.VMEM((CONV_A_WIDTH - 1, D_CONV), F32),
            pltpu.VMEM((SSM_CONV_WIDTH - 1, D_XBC), F32),
            pltpu.VMEM((1, LANES), F32),
        ],
        input_output_aliases=aliases,
        compiler_params=pltpu.CompilerParams(
            dimension_semantics=("arbitrary", "arbitrary"), vmem_limit_bytes=VMEM_LIMIT),
        name="mixer",
    )(*args)


def _mixer_consts(q):
    r = jnp.arange(q)
    tril = (r[:, None] >= r[None, :]).astype(BF16)
    j2 = jnp.arange(2 * LANES) % LANES
    e2 = (j2[:, None] == (jnp.arange(D_SSM) // SSM_HEAD_DIM)[None, :]).astype(BF16)
    c = jnp.arange(D_CONV + D_SSM)
    grp = jnp.where(c < D_CONV, c // (D_CONV // CONV_GROUPS),
                    CONV_GROUPS + (c - D_CONV) // (D_SSM // SSM_GROUPS))
    wgt = jnp.where(c < D_CONV, CONV_GROUPS / D_CONV, SSM_GROUPS / D_SSM)
    g = jnp.where(grp[:, None] == jnp.arange(LANES)[None, :], wgt[:, None], 0.0).astype(BF16)
    e3 = (j2[:, None] == grp[None, :]).astype(BF16)
    return tril, e2, g, e3


def _prep_mixer_weights(norm_mix, w_in, conv_a_w, norm_conv_a, conv_ssm_w, conv_ssm_b, dt_bias, a_log,
                        d_skip, norm_ssm, w_out, norm_ffn, w_router, b_router):
    hpad = LANES - SSM_HEADS
    w_in_p = jnp.pad(w_in[0], ((0, 0), (0, hpad))).astype(BF16)
    wr = jnp.pad(w_router[0], ((0, 0), (0, LANES - N_EXPERTS)))
    wr_hi = wr.astype(BF16)
    wr_mid = (wr - wr_hi.astype(F32)).astype(BF16)
    wr3 = jnp.concatenate([wr_hi, wr_mid, wr_hi], axis=0)
    br = jnp.pad(b_router[0], (0, LANES - N_EXPERTS), constant_values=NEG_BIG)[None]
    return (norm_mix[0][None], w_in_p, conv_a_w[0], norm_conv_a[0][None], conv_ssm_w[0],
            conv_ssm_b[0][None], jnp.pad(dt_bias[0], (0, hpad))[None], jnp.pad(a_log[0], (0, hpad))[None],
            jnp.repeat(d_skip[0], SSM_HEAD_DIM)[None], norm_ssm[0][None], w_out[0].astype(BF16),
            norm_ffn[0][None], wr3, br)


MIX_TL = 512
MIX_Q = 256
COMB_TT = 512


def _routing(top_i, rank, counts):
    t = top_i.shape[1]
    n = t * TOP_K
    assert n % MOE_BLOCK == 0
    nblk = n // MOE_BLOCK
    n_items = nblk + N_EXPERTS
    shift = max(n - 1, 1).bit_length()
    assert N_EXPERTS << shift < 2 ** 31
    ent = (jnp.arange(t, dtype=jnp.int32) * TOP_K)[None, :] + jnp.arange(TOP_K, dtype=jnp.int32)[:, None]
    key = lax.sort((top_i * (1 << shift) + ent).reshape(n))
    slot_tok = (key & ((1 << shift) - 1)) // TOP_K
    group_end = jnp.cumsum(counts)
    group_start = group_end - counts
    pos = rank
    for e in range(N_EXPERTS):
        pos = pos + jnp.where(top_i == e, group_start[e], 0)
    bp = lax.sort(jnp.concatenate([jnp.arange(nblk, dtype=jnp.int32) * MOE_BLOCK, group_start]))
    dup = jnp.concatenate([jnp.zeros((1,), bool), bp[1:] == bp[:-1]])
    bp = lax.sort(jnp.where(jnp.logical_or(dup, bp >= n), n, bp))
    blk = jnp.minimum(bp // MOE_BLOCK, nblk - 1)
    lo = bp - blk * MOE_BLOCK
    exp = jnp.sum((group_end[None, :] <= jnp.minimum(bp, n - 1)[:, None]).astype(jnp.int32), axis=1)
    items = tuple(a.astype(jnp.int32) for a in (blk, exp, lo))
    assert items[0].shape == (n_items,)
    return slot_tok.astype(jnp.int32).reshape(nblk, 1, MOE_BLOCK), pos.astype(jnp.int32), items


def _tile_gather_start(src_hbm, idx_ref, idx_off, n_rows, dst, sem):
    def body(i, carry):
        src_row = pl.multiple_of(idx_ref[0, idx_off + i] * ROW_TILES, ROW_TILES)
        dst_row = pl.multiple_of(i * ROW_TILES, ROW_TILES)
        pltpu.make_async_copy(src_hbm.at[pl.ds(src_row, ROW_TILES), :],
                              dst.at[pl.ds(dst_row, ROW_TILES), :], sem).start()
        return carry
    lax.fori_loop(0, n_rows, body, 0, unroll=8)


def _tile_gather_start_inline(src_hbm, idx_ref, idx_off, n_rows, dst, sem):
    for i in range(n_rows):
        src_row = pl.multiple_of(idx_ref[0, idx_off + i] * ROW_TILES, ROW_TILES)
        pltpu.make_async_copy(src_hbm.at[pl.ds(src_row, ROW_TILES), :],
                              dst.at[pl.ds(i * ROW_TILES, ROW_TILES), :], sem).start(priority=i % 2)


RING = 3


def _expert_kernel(blk_ref, exp_ref, lo_ref, tok_cur, tok_nxt, tok_far, xf_hbm,
                   wg_ref, bg_ref, wu_ref, bu_ref, wd_ref, bd_ref, yb_ref, xbuf, sem, wgb, wub, wdb, *, n_items):
    w = pl.program_id(0)
    lo = lo_ref[w]
    slot = lax.rem(w, RING)
    far = lax.rem(w + RING - 1, RING)

    def wait_rows(s):
        pltpu.make_async_copy(xf_hbm.at[pl.ds(0, MOE_BLOCK * ROW_TILES), :], xbuf.at[s], sem.at[s]).wait()

    @pl.when(w == 0)
    def _():
        _tile_gather_start(xf_hbm, tok_cur, 0, MOE_BLOCK, xbuf.at[0], sem.at[0])
        _tile_gather_start(xf_hbm, tok_nxt, 0, MOE_BLOCK, xbuf.at[1], sem.at[1])

    @pl.when(jnp.logical_or(w == 0, exp_ref[w] != exp_ref[jnp.maximum(w - 1, 0)]))
    def _():
        wgb[...] = wg_ref[...].astype(BF16)
        wub[...] = wu_ref[...].astype(BF16)
        wdb[...] = wd_ref[...].astype(BF16)

    wait_rows(slot)
    xb = _load_token_tiles(xbuf.at[slot], MOE_BLOCK).astype(BF16)
    gate = jnp.dot(xb, wgb[...], preferred_element_type=F32) + bg_ref[...]
    up = jnp.dot(xb, wub[...], preferred_element_type=F32) + bu_ref[...]
    gate = jnp.minimum(gate, SWIGLU_LIMIT)
    up = jnp.clip(up, -SWIGLU_LIMIT, SWIGLU_LIMIT)
    glu = gate * jax.nn.sigmoid(gate * SWIGLU_ALPHA)
    h = ((up + 1.0) * glu).astype(BF16)
    y = jnp.dot(h, wdb[...], preferred_element_type=F32) + bd_ref[...]
    keep = lax.broadcasted_iota(jnp.int32, (MOE_BLOCK, D_MODEL), 0) < lo
    _store_token_tiles(yb_ref, jnp.where(keep, _load_token_tiles(yb_ref, MOE_BLOCK), y))
    _tile_gather_start_inline(xf_hbm, tok_far, 0, MOE_BLOCK, xbuf.at[far], sem.at[far])

    @pl.when(w == n_items - 1)
    def _():
        for s in range(1, RING):
            wait_rows(lax.rem(w + s, RING))


def _expert_call(slot_tok, items, xf, w_gate, b_gate, w_up, b_up, w_down, b_down):
    nblk = slot_tok.shape[0]
    n_items = items[0].shape[0]
    d, de = w_gate.shape[1], w_gate.shape[2]
    tok_spec = lambda f: pl.BlockSpec((None, 1, MOE_BLOCK), f, memory_space=pltpu.SMEM)
    w_spec = lambda r, c: pl.BlockSpec((None, r, c), lambda w, blk, exp, lo: (exp[w], 0, 0))
    grid_spec = pltpu.PrefetchScalarGridSpec(
        num_scalar_prefetch=3,
        grid=(n_items,),
        in_specs=[
            tok_spec(lambda w, blk, exp, lo: (blk[w], 0, 0)),
            tok_spec(lambda w, blk, exp, lo: (blk[jnp.minimum(w + 1, n_items - 1)], 0, 0)),
            tok_spec(lambda w, blk, exp, lo: (blk[jnp.minimum(w + RING - 1, n_items - 1)], 0, 0)),
            pl.BlockSpec(memory_space=pl.ANY),
            w_spec(d, de), w_spec(1, de), w_spec(d, de), w_spec(1, de), w_spec(de, d), w_spec(1, d),
        ],
        out_specs=pl.BlockSpec((MOE_BLOCK * ROW_TILES, LANES), lambda w, blk, exp, lo: (blk[w], 0)),
        scratch_shapes=[
            pltpu.VMEM((RING, MOE_BLOCK * ROW_TILES, LANES), F32),
            pltpu.SemaphoreType.DMA((RING,)),
            pltpu.VMEM((d, de), BF16), pltpu.VMEM((d, de), BF16), pltpu.VMEM((de, d), BF16),
        ],
    )
    return pl.pallas_call(
        functools.partial(_expert_kernel, n_items=n_items),
        grid_spec=grid_spec,
        out_shape=jax.ShapeDtypeStruct((nblk * MOE_BLOCK * ROW_TILES, LANES), F32),
        compiler_params=pltpu.CompilerParams(dimension_semantics=("arbitrary",), vmem_limit_bytes=VMEM_LIMIT),
        name="experts",
    )(*items, slot_tok, slot_tok, slot_tok, xf, w_gate, b_gate[:, None, :], w_up, b_up[:, None, :],
      w_down, b_down[:, None, :])


def _combine_kernel(pos_cur, pos_nxt, pos_far, x1_ref, rrow_ref, yb_hbm, fn_ref, out_ref, ybuf, sem, *, n):
    i = pl.program_id(0)
    slot = lax.rem(i, RING)
    far = lax.rem(i + RING - 1, RING)
    tt = x1_ref.shape[0]

    def wait_rows(s):
        for k in range(TOP_K):
            pltpu.make_async_copy(yb_hbm.at[pl.ds(0, tt * ROW_TILES), :], ybuf.at[s, k], sem.at[s]).wait()

    @pl.when(i == 0)
    def _():
        for k in range(TOP_K):
            _tile_gather_start(yb_hbm, pos_cur, k * tt, tt, ybuf.at[0, k], sem.at[0])
            _tile_gather_start(yb_hbm, pos_nxt, k * tt, tt, ybuf.at[1, k], sem.at[1])

    wait_rows(slot)
    rr = rrow_ref[...]
    acc = x1_ref[...]
    for k in range(TOP_K):
        acc = acc + rr[:, TOP_K + k:TOP_K + k + 1] * _load_token_tiles(ybuf.at[slot, k], tt)
    out_ref[...] = acc * lax.rsqrt(jnp.mean(acc * acc, axis=-1, keepdims=True) + EPS) * fn_ref[...]
    for k in range(TOP_K):
        _tile_gather_start_inline(yb_hbm, pos_far, k * tt, tt, ybuf.at[far, k], sem.at[far])

    @pl.when(i == n - 1)
    def _():
        for s in range(1, RING):
            wait_rows(lax.rem(i + s, RING))


def _combine_call(pos, x1, rrow, yb, fn, *, row_off, n_rows):
    tt = min(COMB_TT, n_rows)
    n_tiles = n_rows // tt
    off = row_off // tt
    d = x1.shape[1]
    pos = pos[:, row_off:row_off + n_rows].reshape(TOP_K, n_tiles, tt)
    pos = pos.transpose(1, 0, 2).reshape(n_tiles, 1, TOP_K * tt)
    pos_spec = lambda f: pl.BlockSpec((None, 1, tt * TOP_K), f, memory_space=pltpu.SMEM)
    return pl.pallas_call(
        functools.partial(_combine_kernel, n=n_tiles),
        grid=(n_tiles,),
        in_specs=[
            pos_spec(lambda i: (i, 0, 0)),
            pos_spec(lambda i: (jnp.minimum(i + 1, n_tiles - 1), 0, 0)),
            pos_spec(lambda i: (jnp.minimum(i + RING - 1, n_tiles - 1), 0, 0)),
            pl.BlockSpec((tt, d), lambda i: (off + i, 0)),
            pl.BlockSpec((tt, LANES), lambda i: (off + i, 0)),
            pl.BlockSpec(memory_space=pl.ANY),
            pl.BlockSpec((1, d), lambda i: (0, 0)),
        ],
        out_specs=pl.BlockSpec((tt, d), lambda i: (i, 0)),
        out_shape=jax.ShapeDtypeStruct((n_rows, d), F32),
        scratch_shapes=[pltpu.VMEM((RING, TOP_K, tt * ROW_TILES, LANES), F32), pltpu.SemaphoreType.DMA((RING,))],
        compiler_params=pltpu.CompilerParams(dimension_semantics=("arbitrary",), vmem_limit_bytes=VMEM_LIMIT),
        name="combine",
    )(pos, pos, pos, x1, rrow, yb, fn)


def kernel(x_prompt, x_sample, cache_conv_a, cache_conv_ssm, state_ssm, norm_mix, w_in, conv_a_w, norm_conv_a, conv_ssm_w, conv_ssm_b, dt_bias, a_log, d_skip, norm_ssm, w_out, norm_ffn, w_router, b_router, w_gate, b_gate, w_up, b_up, w_down, b_down, final_norm):
    bp, lp, _ = x_prompt.shape
    bs, ls, _ = x_sample.shape
    t_p, t_s = bp * lp, bs * ls
    t_all = t_p + t_s
    wts = _prep_mixer_weights(norm_mix, w_in, conv_a_w, norm_conv_a, conv_ssm_w, conv_ssm_b, dt_bias, a_log,
                              d_skip, norm_ssm, w_out, norm_ffn, w_router, b_router)
    zeros = lambda *s: jnp.zeros(s, F32)
    x1, xf, rrow, route_p, ca_p, cs_p, st_p, cnt = _mixer_call(
        x_prompt, zeros(bp, CONV_A_WIDTH - 1, D_CONV), zeros(bp, SSM_CONV_WIDTH - 1, D_XBC),
        zeros(bp, SSM_HEADS, SSM_HEAD_DIM, SSM_STATE), zeros(1, LANES), wts, _mixer_consts(MIX_Q),
        tl=MIX_TL, q=MIX_Q, t_all=t_all, row_off=0)
    x1, xf, rrow, route_s, ca_s, cs_s, st_s, cnt = _mixer_call(
        x_sample, cache_conv_a[0], cache_conv_ssm[0], state_ssm[0], cnt, wts,
        _mixer_consts(ls), tl=ls, q=ls, t_all=t_all, row_off=t_p, alias=(x1, xf, rrow))

    def rows(r0):
        return jnp.concatenate([route_p[:, r0:r0 + TOP_K, :].transpose(1, 0, 2).reshape(TOP_K, t_p),
                                route_s[:, r0:r0 + TOP_K, :].transpose(1, 0, 2).reshape(TOP_K, t_s)],
                               axis=1).astype(jnp.int32)
    slot_tok, pos, items = _routing(rows(0), rows(2 * TOP_K), cnt[0, :N_EXPERTS].astype(jnp.int32))

    yb = _expert_call(slot_tok, items, xf, w_gate[0], b_gate[0], w_up[0], b_up[0], w_down[0], b_down[0])
    fn = final_norm[None]
    y_p = _combine_call(pos, x1, rrow, yb, fn, row_off=0, n_rows=t_p)
    y_s = _combine_call(pos, x1, rrow, yb, fn, row_off=t_p, n_rows=t_s)
    return (y_p.reshape(bp, lp, D_MODEL), y_s.reshape(bs, ls, D_MODEL),
            ca_p[None], cs_p[None], st_p[None], ca_s[None], cs_s[None], st_s[None])
```

```python
import functools

import jax
import jax.numpy as jnp
from jax import lax
from jax.experimental import pallas as pl
from jax.experimental.pallas import tpu as pltpu

F32 = jnp.float32
BF16 = jnp.bfloat16

D_MODEL = 1024
D_CONV = 1024
D_SSM = 1024
CONV_GROUPS = 16
CONV_A_WIDTH = 3
SSM_HEAD_DIM = 64
SSM_HEADS = 16
SSM_GROUPS = 2
SSM_STATE = 128
SSM_CONV_WIDTH = 4
D_XBC = D_SSM + 2 * SSM_GROUPS * SSM_STATE
N_EXPERTS = 32
TOP_K = 4
SWIGLU_LIMIT = 7.0
SWIGLU_ALPHA = 1.702
MOE_BLOCK = 512
EPS = 1e-5

LANES = 128
COL_A = 3 * D_CONV
COL_S = D_SSM + D_XBC + LANES
D_IN_PAD = COL_A + COL_S
assert 2 * SSM_HEAD_DIM == LANES and SSM_STATE == LANES
NEG_BIG = -1e30
VMEM_LIMIT = 56 * 1024 * 1024


def _softplus(v):
    return jnp.maximum(v, 0.0) + jnp.log1p(jnp.exp(-jnp.abs(v)))


def _silu(v):
    return v * jax.nn.sigmoid(v)


def _split2(v):
    hi = v.astype(BF16)
    mid = (v - hi.astype(F32)).astype(BF16)
    return hi, mid


ROUTE_ROWS = 16
ROW_TILES = D_MODEL // LANES


def _store_token_tiles(ref, v, row0=0):
    r = v.shape[0]
    for s in range(ROW_TILES):
        ref[pl.ds(row0 * ROW_TILES + s, r, stride=ROW_TILES), :] = v[:, s * LANES:(s + 1) * LANES]


def _load_token_tiles(ref, r):
    return jnp.concatenate([ref[pl.ds(s, r, stride=ROW_TILES), :] for s in range(ROW_TILES)], axis=1)


def _transpose_rows(v):
    r = v.shape[0]
    if r % LANES:
        v = jnp.concatenate([v, jnp.zeros((LANES - r % LANES, v.shape[1]), v.dtype)], axis=0)
    return v.T[:, :r]


def _ssd_chunk(xs, bm, cm, dt, a_neg, z, dskip, s_scr, tril, e2):
    q = xs.shape[0]
    da = dt * a_neg
    d_hi = da.astype(BF16)
    r1 = da - d_hi.astype(F32)
    d_mid = r1.astype(BF16)
    d_lo = (r1 - d_mid.astype(F32)).astype(BF16)
    cs = jnp.dot(tril, jnp.concatenate([d_hi, d_mid, d_lo], axis=1), preferred_element_type=F32)
    acum = cs[:, 0:LANES] + cs[:, LANES:2 * LANES] + cs[:, 2 * LANES:3 * LANES]
    acum_t = _transpose_rows(acum)
    total = acum[q - 1:q, :]
    stack = jnp.concatenate(
        [dt, jnp.exp(total - acum), jnp.exp(acum), jnp.broadcast_to(jnp.exp(total), (8, LANES))], axis=0)
    s_hi, s_mid = _split2(stack)
    ex = jnp.dot(jnp.concatenate([s_hi, s_mid], axis=1), e2, preferred_element_type=F32)
    dt_e, dec_e, ea_e, cd_e = ex[0:q], ex[q:2 * q], ex[2 * q:3 * q], ex[3 * q:3 * q + 1]
    xdt = xs * dt_e
    xdt_b = xdt.astype(BF16)
    xdec_b = (xdt * dec_e).astype(BF16)
    row = lax.broadcasted_iota(jnp.int32, (q, q), 0)
    col = lax.broadcasted_iota(jnp.int32, (q, q), 1)
    causal = row >= col
    lane = lax.broadcasted_iota(jnp.int32, (q, LANES), 1)
    gw = D_SSM // SSM_GROUPS
    hpg = SSM_HEADS // SSM_GROUPS
    ys = []
    for g in range(SSM_GROUPS):
        cg = cm[:, g * SSM_STATE:(g + 1) * SSM_STATE].astype(BF16)
        bg = bm[:, g * SSM_STATE:(g + 1) * SSM_STATE].astype(BF16)
        cb = lax.dot_general(cg, bg, (((1,), (1,)), ((), ())), preferred_element_type=F32)
        s_g = s_scr[g]
        y_off = jnp.dot(cg, s_g.astype(BF16), preferred_element_type=F32) * ea_e[:, g * gw:(g + 1) * gw]
        pieces = []
        for j in range(hpg // 2):
            h0 = g * hpg + 2 * j
            ms = []
            for h in (h0, h0 + 1):
                diff = acum[:, h:h + 1] - acum_t[h:h + 1, :]
                ms.append(cb * jnp.exp(jnp.where(causal, diff, NEG_BIG)))
            m = jnp.concatenate(ms, axis=1).astype(BF16)
            xp = xdt_b[:, h0 * SSM_HEAD_DIM:(h0 + 2) * SSM_HEAD_DIM]
            zero = jnp.zeros_like(xp)
            rhs = jnp.concatenate([jnp.where(lane < SSM_HEAD_DIM, xp, zero),
                                   jnp.where(lane >= SSM_HEAD_DIM, xp, zero)], axis=0)
            pieces.append(jnp.dot(m, rhs, preferred_element_type=F32))
        ys.append(jnp.concatenate(pieces, axis=1) + y_off)
        upd = lax.dot_general(bg, xdec_b[:, g * gw:(g + 1) * gw], (((0,), (0,)), ((), ())),
                              preferred_element_type=F32)
        s_scr[g] = s_g * cd_e[:, g * gw:(g + 1) * gw] + upd
    y = jnp.concatenate(ys, axis=1) + dskip * xs
    return y * _silu(z)


def _mixer_kernel(*refs, tl, q, nl, n_alias):
    (x_ref, ca0_ref, cs0_ref, ssm0_ref, cnt0_ref, nmix_ref, win_ref, caw_ref, na_ref, csw_ref, csb_ref,
     dtb_ref, alog_ref, dsk_ref, ns_ref, wout_ref, nffn_ref, wr_ref, br_ref,
     tril_ref, e2_ref, g_ref, e3_ref) = refs[:23]
    refs = refs[23 + n_alias:]
    (x1_ref, xf_ref, rrow_ref, route_ref, ca_ref, cs_ref, ssm_ref, cnt_ref,
     upad, xpad, s_scr, ca_scr, cs_scr, cnt_scr) = refs
    l = pl.program_id(1)
    last = nl - 1
    pa_rows = CONV_A_WIDTH - 1
    ps_rows = SSM_CONV_WIDTH - 1

    @pl.when(jnp.logical_and(pl.program_id(0) == 0, l == 0))
    def _():
        cnt_scr[...] = cnt0_ref[...]

    @pl.when(l == 0)
    def _():
        ca_scr[...] = ca0_ref[...]
        cs_scr[...] = cs0_ref[...]
        for hp in range(SSM_HEADS // 2):
            g, j = divmod(hp, SSM_HEADS // SSM_GROUPS // 2)
            pair = jnp.concatenate([ssm0_ref[2 * hp], ssm0_ref[2 * hp + 1]], axis=0)
            s_scr[g, :, j * LANES:(j + 1) * LANES] = pair.T

    caw = caw_ref[...]
    csw = csw_ref[...]
    a_neg = -jnp.exp(alog_ref[...])
    gam = jnp.concatenate([na_ref[...], ns_ref[...]], axis=1)
    nb = D_SSM + SSM_GROUPS * SSM_STATE
    lane = lax.broadcasted_iota(jnp.int32, (q, LANES), 1)
    lane_f = lane.astype(F32)
    tail_a = ca_scr[...]
    tail_s = cs_scr[...]
    for j in range(tl // q):
        rows = slice(j * q, (j + 1) * q)
        x = x_ref[rows, :]
        xn = (x * lax.rsqrt(jnp.mean(x * x, axis=-1, keepdims=True) + EPS) * nmix_ref[...]).astype(BF16)

        pa = jnp.dot(xn, win_ref[:, 0:COL_A], preferred_element_type=F32)
        a_b, a_c, a_h = pa[:, 0:D_CONV], pa[:, D_CONV:2 * D_CONV], pa[:, 2 * D_CONV:3 * D_CONV]
        u = a_c * a_h
        upad[j, 8 - pa_rows:8, :] = tail_a
        upad[j, 8:8 + q, :] = u
        conv = upad[j, 8 - pa_rows:8 - pa_rows + q, :] * caw[0:1, :]
        for k in range(1, CONV_A_WIDTH):
            conv = conv + upad[j, 8 - pa_rows + k:8 - pa_rows + k + q, :] * caw[k:k + 1, :]
        y_a = a_b * conv
        tail_a = u[q - pa_rows:q, :]

        ps = jnp.dot(xn, win_ref[:, COL_A:D_IN_PAD], preferred_element_type=F32)
        z = ps[:, 0:D_SSM]
        xbc = ps[:, D_SSM:D_SSM + D_XBC]
        xpad[j, 8 - ps_rows:8, :] = tail_s
        xpad[j, 8:8 + q, :] = xbc
        dt = _softplus(ps[:, D_SSM + D_XBC:COL_S] + dtb_ref[...])
        acc = xpad[j, 8 - ps_rows:8 - ps_rows + q, :] * csw[0:1, :]
        for k in range(1, SSM_CONV_WIDTH):
            acc = acc + xpad[j, 8 - ps_rows + k:8 - ps_rows + k + q, :] * csw[k:k + 1, :]
        xc = _silu(acc + csb_ref[...])
        tail_s = xbc[q - ps_rows:q, :]
        y_s = _ssd_chunk(xc[:, 0:D_SSM], xc[:, D_SSM:nb], xc[:, nb:D_XBC], dt, a_neg, z, dsk_ref[...],
                         s_scr, tril_ref[...], e2_ref[...])

        mixed = jnp.concatenate([y_a, y_s], axis=1)
        mean = jnp.dot((mixed * mixed).astype(BF16), g_ref[...], preferred_element_type=F32)
        r_hi, r_mid = _split2(lax.rsqrt(mean + EPS))
        rs_e = jnp.dot(jnp.concatenate([r_hi, r_mid], axis=1), e3_ref[...], preferred_element_type=F32)
        mixed_n = (mixed * rs_e * gam).astype(BF16)
        x1 = x + jnp.dot(mixed_n, wout_ref[...], preferred_element_type=F32)
        x1_ref[rows, :] = x1

        xf = x1 * lax.rsqrt(jnp.mean(x1 * x1, axis=-1, keepdims=True) + EPS) * nffn_ref[...]
        _store_token_tiles(xf_ref, xf, row0=j * q)
        f_hi, f_mid = _split2(xf)
        logits = jnp.dot(jnp.concatenate([f_hi, f_hi, f_mid], axis=1), wr_ref[...],
                         preferred_element_type=F32) + br_ref[...]
        work = logits
        vals, idxs = [], []
        for _ in range(TOP_K):
            m = jnp.max(work, axis=-1, keepdims=True)
            idx = jnp.min(jnp.where(work == m, lane_f, float(LANES)), axis=-1, keepdims=True)
            vals.append(m)
            idxs.append(idx)
            work = jnp.where(lane_f == idx, -jnp.inf, work)
        es = [jnp.exp(v - vals[0]) for v in vals]
        inv = 1.0 / (es[0] + es[1] + es[2] + es[3])
        hots = [(lane_f == idxs[k]).astype(F32) for k in range(TOP_K)]
        hot = hots[0] + hots[1] + hots[2] + hots[3]
        before = jnp.dot(tril_ref[...], hot.astype(BF16), preferred_element_type=F32) - hot + cnt_scr[...]
        cnt_scr[...] = before[q - 1:q, :] + hot[q - 1:q, :]
        route = jnp.zeros((q, LANES), F32)
        for k in range(TOP_K):
            route = jnp.where(lane == k, idxs[k], route)
            route = jnp.where(lane == TOP_K + k, es[k] * inv, route)
            route = jnp.where(lane == 2 * TOP_K + k, jnp.sum(hots[k] * before, axis=-1, keepdims=True), route)
        rrow_ref[rows, :] = route
        route_ref[:, rows] = _transpose_rows(route)[0:ROUTE_ROWS, :]

    ca_scr[...] = tail_a
    cs_scr[...] = tail_s
    cnt_ref[...] = cnt_scr[...]

    @pl.when(l == last)
    def _():
        ca_ref[...] = tail_a
        cs_ref[...] = tail_s
        for hp in range(SSM_HEADS // 2):
            g, j = divmod(hp, SSM_HEADS // SSM_GROUPS // 2)
            pair = s_scr[g, :, j * LANES:(j + 1) * LANES].T
            ssm_ref[2 * hp] = pair[0:SSM_HEAD_DIM, :]
            ssm_ref[2 * hp + 1] = pair[SSM_HEAD_DIM:2 * SSM_HEAD_DIM, :]


def _const_spec(shape):
    nd = len(shape)
    return pl.BlockSpec(shape, lambda b, l: (0,) * nd, pipeline_mode=pl.Buffered(1))


def _mixer_call(x, ca0, cs0, ssm0, cnt0, wts, consts, *, tl, q, t_all, row_off, alias=None):
    b_sz, l_sz, _ = x.shape
    nl = l_sz // tl
    blk_off = row_off // tl
    n_alias = 0 if alias is None else len(alias)
    in_specs = [
        pl.BlockSpec((None, tl, D_MODEL), lambda b, l: (b, l, 0)),
        pl.BlockSpec((None, CONV_A_WIDTH - 1, D_CONV), lambda b, l: (b, 0, 0)),
        pl.BlockSpec((None, SSM_CONV_WIDTH - 1, D_XBC), lambda b, l: (b, 0, 0)),
        pl.BlockSpec((None, SSM_HEADS, SSM_HEAD_DIM, SSM_STATE), lambda b, l: (b, 0, 0, 0)),
        _const_spec(cnt0.shape),
    ] + [_const_spec(w.shape) for w in wts] + [_const_spec(c.shape) for c in consts]
    in_specs += [pl.BlockSpec(memory_space=pl.ANY)] * n_alias
    row_spec = pl.BlockSpec((tl, D_MODEL), lambda b, l: (blk_off + b * nl + l, 0))
    out_specs = [
        row_spec,
        pl.BlockSpec((tl * ROW_TILES, LANES), lambda b, l: (blk_off + b * nl + l, 0)),
        pl.BlockSpec((tl, LANES), lambda b, l: (blk_off + b * nl + l, 0)),
        pl.BlockSpec((None, ROUTE_ROWS, tl), lambda b, l: (b, 0, l)),
        pl.BlockSpec((None, CONV_A_WIDTH - 1, D_CONV), lambda b, l: (b, 0, 0)),
        pl.BlockSpec((None, SSM_CONV_WIDTH - 1, D_XBC), lambda b, l: (b, 0, 0)),
        pl.BlockSpec((None, SSM_HEADS, SSM_HEAD_DIM, SSM_STATE), lambda b, l: (b, 0, 0, 0)),
        pl.BlockSpec((1, LANES), lambda b, l: (0, 0)),
    ]
    out_shape = [
        jax.ShapeDtypeStruct((t_all, D_MODEL), F32),
        jax.ShapeDtypeStruct((t_all * ROW_TILES, LANES), F32),
        jax.ShapeDtypeStruct((t_all, LANES), F32),
        jax.ShapeDtypeStruct((b_sz, ROUTE_ROWS, l_sz), F32),
        jax.ShapeDtypeStruct((b_sz, CONV_A_WIDTH - 1, D_CONV), F32),
        jax.ShapeDtypeStruct((b_sz, SSM_CONV_WIDTH - 1, D_XBC), F32),
        jax.ShapeDtypeStruct((b_sz, SSM_HEADS, SSM_HEAD_DIM, SSM_STATE), F32),
        jax.ShapeDtypeStruct((1, LANES), F32),
    ]
    n_in = 5 + len(wts) + len(consts)
    aliases = {n_in + i: i for i in range(n_alias)}
    args = [x, ca0, cs0, ssm0, cnt0, *wts, *consts] + ([] if alias is None else list(alias))
    return pl.pallas_call(
        functools.partial(_mixer_kernel, tl=tl, q=q, nl=nl, n_alias=n_alias),
        grid=(b_sz, nl),
        in_specs=in_specs,
        out_specs=out_specs,
        out_shape=out_shape,
        scratch_shapes=[
            pltpu.VMEM((tl // q, q + 8, D_CONV), F32),
            pltpu.VMEM((tl // q, q + 8, D_XBC), F32),
            pltpu.VMEM((SSM_GROUPS, SSM_STATE, D_SSM // SSM_GROUPS), F32),
            pltpu.VMEM((CONV_A_WIDTH - 1, D_CONV), F32),
            pltpu.VMEM((SSM_CONV_WIDTH - 1, D_XBC), F32),
            pltpu.VMEM((1, LANES), F32),
        ],
        input_output_aliases=aliases,
        compiler_params=pltpu.CompilerParams(
            dimension_semantics=("arbitrary", "arbitrary"), vmem_limit_bytes=VMEM_LIMIT),
        name="mixer",
    )(*args)


def _mixer_consts(q):
    r = jnp.arange(q)
    tril = (r[:, None] >= r[None, :]).astype(BF16)
    j2 = jnp.arange(2 * LANES) % LANES
    e2 = (j2[:, None] == (jnp.arange(D_SSM) // SSM_HEAD_DIM)[None, :]).astype(BF16)
    c = jnp.arange(D_CONV + D_SSM)
    grp = jnp.where(c < D_CONV, c // (D_CONV // CONV_GROUPS),
                    CONV_GROUPS + (c - D_CONV) // (D_SSM // SSM_GROUPS))
    wgt = jnp.where(c < D_CONV, CONV_GROUPS / D_CONV, SSM_GROUPS / D_SSM)
    g = jnp.where(grp[:, None] == jnp.arange(LANES)[None, :], wgt[:, None], 0.0).astype(BF16)
    e3 = (j2[:, None] == grp[None, :]).astype(BF16)
    return tril, e2, g, e3


def _prep_mixer_weights(norm_mix, w_in, conv_a_w, norm_conv_a, conv_ssm_w, conv_ssm_b, dt_bias, a_log,
                        d_skip, norm_ssm, w_out, norm_ffn, w_router, b_router):
    hpad = LANES - SSM_HEADS
    w_in_p = jnp.pad(w_in[0], ((0, 0), (0, hpad))).astype(BF16)
    wr = jnp.pad(w_router[0], ((0, 0), (0, LANES - N_EXPERTS)))
    wr_hi = wr.astype(BF16)
    wr_mid = (wr - wr_hi.astype(F32)).astype(BF16)
    wr3 = jnp.concatenate([wr_hi, wr_mid, wr_hi], axis=0)
    br = jnp.pad(b_router[0], (0, LANES - N_EXPERTS), constant_values=NEG_BIG)[None]
    return (norm_mix[0][None], w_in_p, conv_a_w[0], norm_conv_a[0][None], conv_ssm_w[0],
            conv_ssm_b[0][None], jnp.pad(dt_bias[0], (0, hpad))[None], jnp.pad(a_log[0], (0, hpad))[None],
            jnp.repeat(d_skip[0], SSM_HEAD_DIM)[None], norm_ssm[0][None], w_out[0].astype(BF16),
            norm_ffn[0][None], wr3, br)


MIX_TL = 512
MIX_Q = 256
COMB_TT = 256


def _routing(top_i, rank, counts):
    t = top_i.shape[1]
    n = t * TOP_K
    assert n % MOE_BLOCK == 0
    nblk = n // MOE_BLOCK
    n_items = nblk + N_EXPERTS
    shift = max(n - 1, 1).bit_length()
    assert N_EXPERTS << shift < 2 ** 31
    ent = (jnp.arange(t, dtype=jnp.int32) * TOP_K)[None, :] + jnp.arange(TOP_K, dtype=jnp.int32)[:, None]
    key = lax.sort((top_i * (1 << shift) + ent).reshape(n))
    slot_tok = (key & ((1 << shift) - 1)) // TOP_K
    group_end = jnp.cumsum(counts)
    group_start = group_end - counts
    pos = rank
    for e in range(N_EXPERTS):
        pos = pos + jnp.where(top_i == e, group_start[e], 0)
    bp = lax.sort(jnp.concatenate([jnp.arange(nblk, dtype=jnp.int32) * MOE_BLOCK, group_start]))
    dup = jnp.concatenate([jnp.zeros((1,), bool), bp[1:] == bp[:-1]])
    bp = lax.sort(jnp.where(jnp.logical_or(dup, bp >= n), n, bp))
    blk = jnp.minimum(bp // MOE_BLOCK, nblk - 1)
    lo = bp - blk * MOE_BLOCK
    exp = jnp.sum((group_end[None, :] <= jnp.minimum(bp, n - 1)[:, None]).astype(jnp.int32), axis=1)
    items = tuple(a.astype(jnp.int32) for a in (blk, exp, lo))
    assert items[0].shape == (n_items,)
    return slot_tok.astype(jnp.int32).reshape(nblk, 1, MOE_BLOCK), pos.astype(jnp.int32), items


def _tile_gather_start(src_hbm, idx_ref, idx_off, n_rows, dst, sem):
    def body(i, carry):
        src_row = pl.multiple_of(idx_ref[0, idx_off + i] * ROW_TILES, ROW_TILES)
        dst_row = pl.multiple_of(i * ROW_TILES, ROW_TILES)
        pltpu.make_async_copy(src_hbm.at[pl.ds(src_row, ROW_TILES), :],
                              dst.at[pl.ds(dst_row, ROW_TILES), :], sem).start()
        return carry
    lax.fori_loop(0, n_rows, body, 0, unroll=8)


def _tile_gather_start_inline(src_hbm, idx_ref, idx_off, n_rows, dst, sem):
    for i in range(n_rows):
        src_row = pl.multiple_of(idx_ref[0, idx_off + i] * ROW_TILES, ROW_TILES)
        pltpu.make_async_copy(src_hbm.at[pl.ds(src_row, ROW_TILES), :],
                              dst.at[pl.ds(i * ROW_TILES, ROW_TILES), :], sem).start(priority=i % 2)


RING = 3


def _expert_kernel(blk_ref, exp_ref, lo_ref, tok_cur, tok_nxt, tok_far, xf_hbm,
                   wg_ref, bg_ref, wu_ref, bu_ref, wd_ref, bd_ref, yb_ref, xbuf, sem, wgb, wub, wdb, *, n_items):
    w = pl.program_id(0)
    lo = lo_ref[w]
    slot = lax.rem(w, RING)
    far = lax.rem(w + RING - 1, RING)

    def wait_rows(s):
        pltpu.make_async_copy(xf_hbm.at[pl.ds(0, MOE_BLOCK * ROW_TILES), :], xbuf.at[s], sem.at[s]).wait()

    @pl.when(w == 0)
    def _():
        _tile_gather_start(xf_hbm, tok_cur, 0, MOE_BLOCK, xbuf.at[0], sem.at[0])
        _tile_gather_start(xf_hbm, tok_nxt, 0, MOE_BLOCK, xbuf.at[1], sem.at[1])

    @pl.when(jnp.logical_or(w == 0, exp_ref[w] != exp_ref[jnp.maximum(w - 1, 0)]))
    def _():
        wgb[...] = wg_ref[...].astype(BF16)
        wub[...] = wu_ref[...].astype(BF16)
        wdb[...] = wd_ref[...].astype(BF16)

    wait_rows(slot)
    xb = _load_token_tiles(xbuf.at[slot], MOE_BLOCK).astype(BF16)
    gate = jnp.dot(xb, wgb[...], preferred_element_type=F32) + bg_ref[...]
    up = jnp.dot(xb, wub[...], preferred_element_type=F32) + bu_ref[...]
    gate = jnp.minimum(gate, SWIGLU_LIMIT)
    up = jnp.clip(up, -SWIGLU_LIMIT, SWIGLU_LIMIT)
    glu = gate * jax.nn.sigmoid(gate * SWIGLU_ALPHA)
    h = ((up + 1.0) * glu).astype(BF16)
    y = jnp.dot(h, wdb[...], preferred_element_type=F32) + bd_ref[...]
    keep = lax.broadcasted_iota(jnp.int32, (MOE_BLOCK, D_MODEL), 0) < lo
    _store_token_tiles(yb_ref, jnp.where(keep, _load_token_tiles(yb_ref, MOE_BLOCK), y))
    _tile_gather_start_inline(xf_hbm, tok_far, 0, MOE_BLOCK, xbuf.at[far], sem.at[far])

    @pl.when(w == n_items - 1)
    def _():
        for s in range(1, RING):
            wait_rows(lax.rem(w + s, RING))


def _expert_call(slot_tok, items, xf, w_gate, b_gate, w_up, b_up, w_down, b_down):
    nblk = slot_tok.shape[0]
    n_items = items[0].shape[0]
    d, de = w_gate.shape[1], w_gate.shape[2]
    tok_spec = lambda f: pl.BlockSpec((None, 1, MOE_BLOCK), f, memory_space=pltpu.SMEM)
    w_spec = lambda r, c: pl.BlockSpec((None, r, c), lambda w, blk, exp, lo: (exp[w], 0, 0))
    grid_spec = pltpu.PrefetchScalarGridSpec(
        num_scalar_prefetch=3,
        grid=(n_items,),
        in_specs=[
            tok_spec(lambda w, blk, exp, lo: (blk[w], 0, 0)),
            tok_spec(lambda w, blk, exp, lo: (blk[jnp.minimum(w + 1, n_items - 1)], 0, 0)),
            tok_spec(lambda w, blk, exp, lo: (blk[jnp.minimum(w + RING - 1, n_items - 1)], 0, 0)),
            pl.BlockSpec(memory_space=pl.ANY),
            w_spec(d, de), w_spec(1, de), w_spec(d, de), w_spec(1, de), w_spec(de, d), w_spec(1, d),
        ],
        out_specs=pl.BlockSpec((MOE_BLOCK * ROW_TILES, LANES), lambda w, blk, exp, lo: (blk[w], 0)),
        scratch_shapes=[
            pltpu.VMEM((RING, MOE_BLOCK * ROW_TILES, LANES), F32),
            pltpu.SemaphoreType.DMA((RING,)),
            pltpu.VMEM((d, de), BF16), pltpu.VMEM((d, de), BF16), pltpu.VMEM((de, d), BF16),
        ],
    )
    return pl.pallas_call(
        functools.partial(_expert_kernel, n_items=n_items),
        grid_spec=grid_spec,
        out_shape=jax.ShapeDtypeStruct((nblk * MOE_BLOCK * ROW_TILES, LANES), F32),
        compiler_params=pltpu.CompilerParams(dimension_semantics=("arbitrary",), vmem_limit_bytes=VMEM_LIMIT),
        name="experts",
    )(*items, slot_tok, slot_tok, slot_tok, xf, w_gate, b_gate[:, None, :], w_up, b_up[:, None, :],
      w_down, b_down[:, None, :])


def _combine_kernel(pos_cur, pos_nxt, pos_far, x1_ref, rrow_ref, yb_hbm, fn_ref, out_ref, ybuf, sem, *, n):
    i = pl.program_id(0)
    slot = lax.rem(i, RING)
    far = lax.rem(i + RING - 1, RING)
    tt = x1_ref.shape[0]

    def wait_rows(s):
        for k in range(TOP_K):
            pltpu.make_async_copy(yb_hbm.at[pl.ds(0, tt * ROW_TILES), :], ybuf.at[s, k], sem.at[s]).wait()

    @pl.when(i == 0)
    def _():
        for k in range(TOP_K):
            _tile_gather_start(yb_hbm, pos_cur, k * tt, tt, ybuf.at[0, k], sem.at[0])
            _tile_gather_start(yb_hbm, pos_nxt, k * tt, tt, ybuf.at[1, k], sem.at[1])

    wait_rows(slot)
    rr = rrow_ref[...]
    acc = x1_ref[...]
    for k in range(TOP_K):
        acc = acc + rr[:, TOP_K + k:TOP_K + k + 1] * _load_token_tiles(ybuf.at[slot, k], tt)
    out_ref[...] = acc * lax.rsqrt(jnp.mean(acc * acc, axis=-1, keepdims=True) + EPS) * fn_ref[...]
    for k in range(TOP_K):
        _tile_gather_start_inline(yb_hbm, pos_far, k * tt, tt, ybuf.at[far, k], sem.at[far])

    @pl.when(i == n - 1)
    def _():
        for s in range(1, RING):
            wait_rows(lax.rem(i + s, RING))


def _combine_call(pos, x1, rrow, yb, fn, *, row_off, n_rows):
    tt = min(COMB_TT, n_rows)
    n_tiles = n_rows // tt
    off = row_off // tt
    d = x1.shape[1]
    pos = pos[:, row_off:row_off + n_rows].reshape(TOP_K, n_tiles, tt)
    pos = pos.transpose(1, 0, 2).reshape(n_tiles, 1, TOP_K * tt)
    pos_spec = lambda f: pl.BlockSpec((None, 1, tt * TOP_K), f, memory_space=pltpu.SMEM)
    return pl.pallas_call(
        functools.partial(_combine_kernel, n=n_tiles),
        grid=(n_tiles,),
        in_specs=[
            pos_spec(lambda i: (i, 0, 0)),
            pos_spec(lambda i: (jnp.minimum(i + 1, n_tiles - 1), 0, 0)),
            pos_spec(lambda i: (jnp.minimum(i + RING - 1, n_tiles - 1), 0, 0)),
            pl.BlockSpec((tt, d), lambda i: (off + i, 0)),
            pl.BlockSpec((tt, LANES), lambda i: (off + i, 0)),
            pl.BlockSpec(memory_space=pl.ANY),
            pl.BlockSpec((1, d), lambda i: (0, 0)),
        ],
        out_specs=pl.BlockSpec((tt, d), lambda i: (i, 0)),
        out_shape=jax.ShapeDtypeStruct((n_rows, d), F32),
        scratch_shapes=[pltpu.VMEM((RING, TOP_K, tt * ROW_TILES, LANES), F32), pltpu.SemaphoreType.DMA((RING,))],
        compiler_params=pltpu.CompilerParams(dimension_semantics=("arbitrary",), vmem_limit_bytes=VMEM_LIMIT),
        name="combine",
    )(pos, pos, pos, x1, rrow, yb, fn)


def kernel(x_prompt, x_sample, cache_conv_a, cache_conv_ssm, state_ssm, norm_mix, w_in, conv_a_w, norm_conv_a, conv_ssm_w, conv_ssm_b, dt_bias, a_log, d_skip, norm_ssm, w_out, norm_ffn, w_router, b_router, w_gate, b_gate, w_up, b_up, w_down, b_down, final_norm):
    bp, lp, _ = x_prompt.shape
    bs, ls, _ = x_sample.shape
    t_p, t_s = bp * lp, bs * ls
    t_all = t_p + t_s
    wts = _prep_mixer_weights(norm_mix, w_in, conv_a_w, norm_conv_a, conv_ssm_w, conv_ssm_b, dt_bias, a_log,
                              d_skip, norm_ssm, w_out, norm_ffn, w_router, b_router)
    zeros = lambda *s: jnp.zeros(s, F32)
    x1, xf, rrow, route_p, ca_p, cs_p, st_p, cnt = _mixer_call(
        x_prompt, zeros(bp, CONV_A_WIDTH - 1, D_CONV), zeros(bp, SSM_CONV_WIDTH - 1, D_XBC),
        zeros(bp, SSM_HEADS, SSM_HEAD_DIM, SSM_STATE), zeros(1, LANES), wts, _mixer_consts(MIX_Q),
        tl=MIX_TL, q=MIX_Q, t_all=t_all, row_off=0)
    x1, xf, rrow, route_s, ca_s, cs_s, st_s, cnt = _mixer_call(
        x_sample, cache_conv_a[0], cache_conv_ssm[0], state_ssm[0], cnt, wts,
        _mixer_consts(ls), tl=ls, q=ls, t_all=t_all, row_off=t_p, alias=(x1, xf, rrow))

    def rows(r0):
        return jnp.concatenate([route_p[:, r0:r0 + TOP_K, :].transpose(1, 0, 2).reshape(TOP_K, t_p),
                                route_s[:, r0:r0 + TOP_K, :].transpose(1, 0, 2).reshape(TOP_K, t_s)],
                               axis=1).astype(jnp.int32)
    slot_tok, pos, items = _routing(rows(0), rows(2 * TOP_K), cnt[0, :N_EXPERTS].astype(jnp.int32))

    yb = _expert_call(slot_tok, items, xf, w_gate[0], b_gate[0], w_up[0], b_up[0], w_down[0], b_down[0])
    fn = final_norm[None]
    y_p = _combine_call(pos, x1, rrow, yb, fn, row_off=0, n_rows=t_p)
    y_s = _combine_call(pos, x1, rrow, yb, fn, row_off=t_p, n_rows=t_s)
    return (y_p.reshape(bp, lp, D_MODEL), y_s.reshape(bs, ls, D_MODEL),
            ca_p[None], cs_p[None], st_p[None], ca_s[None], cs_s[None], st_s[None])
```

```python
import functools

import jax
import jax.numpy as jnp
from jax import lax
from jax.experimental import pallas as pl
from jax.experimental.pallas import tpu as pltpu

F32 = jnp.float32
BF16 = jnp.bfloat16

D_MODEL = 1024
D_CONV = 1024
D_SSM = 1024
CONV_GROUPS = 16
CONV_A_WIDTH = 3
SSM_HEAD_DIM = 64
SSM_HEADS = 16
SSM_GROUPS = 2
SSM_STATE = 128
SSM_CONV_WIDTH = 4
D_XBC = D_SSM + 2 * SSM_GROUPS * SSM_STATE
N_EXPERTS = 32
TOP_K = 4
SWIGLU_LIMIT = 7.0
SWIGLU_ALPHA = 1.702
MOE_BLOCK = 512
EPS = 1e-5

LANES = 128
COL_A = 3 * D_CONV
COL_S = D_SSM + D_XBC + LANES
D_IN_PAD = COL_A + COL_S
assert 2 * SSM_HEAD_DIM == LANES and SSM_STATE == LANES
NEG_BIG = -1e30
VMEM_LIMIT = 56 * 1024 * 1024


def _softplus(v):
    return jnp.maximum(v, 0.0) + jnp.log1p(jnp.exp(-jnp.abs(v)))


def _silu(v):
    return v * jax.nn.sigmoid(v)


def _split2(v):
    hi = v.astype(BF16)
    mid = (v - hi.astype(F32)).astype(BF16)
    return hi, mid


ROUTE_ROWS = 16
ROW_TILES = D_MODEL // LANES


def _store_token_tiles(ref, v, row0=0):
    r = v.shape[0]
    for s in range(ROW_TILES):
        ref[pl.ds(row0 * ROW_TILES + s, r, stride=ROW_TILES), :] = v[:, s * LANES:(s + 1) * LANES]


def _load_token_tiles(ref, r):
    return jnp.concatenate([ref[pl.ds(s, r, stride=ROW_TILES), :] for s in range(ROW_TILES)], axis=1)


def _transpose_rows(v):
    r = v.shape[0]
    if r % LANES:
        v = jnp.concatenate([v, jnp.zeros((LANES - r % LANES, v.shape[1]), v.dtype)], axis=0)
    return v.T[:, :r]


def _ssd_chunk(xs, bm, cm, dt, a_neg, z, dskip, s_scr, tril, e2):
    q = xs.shape[0]
    da = dt * a_neg
    d_hi = da.astype(BF16)
    r1 = da - d_hi.astype(F32)
    d_mid = r1.astype(BF16)
    d_lo = (r1 - d_mid.astype(F32)).astype(BF16)
    cs = jnp.dot(tril, jnp.concatenate([d_hi, d_mid, d_lo], axis=1), preferred_element_type=F32)
    acum = cs[:, 0:LANES] + cs[:, LANES:2 * LANES] + cs[:, 2 * LANES:3 * LANES]
    acum_t = _transpose_rows(acum)
    total = acum[q - 1:q, :]
    stack = jnp.concatenate(
        [dt, jnp.exp(total - acum), jnp.exp(acum), jnp.broadcast_to(jnp.exp(total), (8, LANES))], axis=0)
    s_hi, s_mid = _split2(stack)
    ex = jnp.dot(jnp.concatenate([s_hi, s_mid], axis=1), e2, preferred_element_type=F32)
    dt_e, dec_e, ea_e, cd_e = ex[0:q], ex[q:2 * q], ex[2 * q:3 * q], ex[3 * q:3 * q + 1]
    xdt = xs * dt_e
    xdt_b = xdt.astype(BF16)
    xdec_b = (xdt * dec_e).astype(BF16)
    row = lax.broadcasted_iota(jnp.int32, (q, q), 0)
    col = lax.broadcasted_iota(jnp.int32, (q, q), 1)
    causal = row >= col
    lane = lax.broadcasted_iota(jnp.int32, (q, LANES), 1)
    gw = D_SSM // SSM_GROUPS
    hpg = SSM_HEADS // SSM_GROUPS
    ys = []
    for g in range(SSM_GROUPS):
        cg = cm[:, g * SSM_STATE:(g + 1) * SSM_STATE].astype(BF16)
        bg = bm[:, g * SSM_STATE:(g + 1) * SSM_STATE].astype(BF16)
        cb = lax.dot_general(cg, bg, (((1,), (1,)), ((), ())), preferred_element_type=F32)
        s_g = s_scr[g]
        y_off = jnp.dot(cg, s_g.astype(BF16), preferred_element_type=F32) * ea_e[:, g * gw:(g + 1) * gw]
        pieces = []
        for j in range(hpg // 2):
            h0 = g * hpg + 2 * j
            ms = []
            for h in (h0, h0 + 1):
                diff = acum[:, h:h + 1] - acum_t[h:h + 1, :]
                ms.append(cb * jnp.exp(jnp.where(causal, diff, NEG_BIG)))
            m = jnp.concatenate(ms, axis=1).astype(BF16)
            xp = xdt_b[:, h0 * SSM_HEAD_DIM:(h0 + 2) * SSM_HEAD_DIM]
            zero = jnp.zeros_like(xp)
            rhs = jnp.concatenate([jnp.where(lane < SSM_HEAD_DIM, xp, zero),
                                   jnp.where(lane >= SSM_HEAD_DIM, xp, zero)], axis=0)
            pieces.append(jnp.dot(m, rhs, preferred_element_type=F32))
        ys.append(jnp.concatenate(pieces, axis=1) + y_off)
        upd = lax.dot_general(bg, xdec_b[:, g * gw:(g + 1) * gw], (((0,), (0,)), ((), ())),
                              preferred_element_type=F32)
        s_scr[g] = s_g * cd_e[:, g * gw:(g + 1) * gw] + upd
    y = jnp.concatenate(ys, axis=1) + dskip * xs
    return y * _silu(z)


def _mixer_kernel(*refs, tl, q, nl, n_alias):
    (x_ref, ca0_ref, cs0_ref, ssm0_ref, cnt0_ref, nmix_ref, win_ref, caw_ref, na_ref, csw_ref, csb_ref,
     dtb_ref, alog_ref, dsk_ref, ns_ref, wout_ref, nffn_ref, wr_ref, br_ref,
     tril_ref, e2_ref, g_ref, e3_ref) = refs[:23]
    refs = refs[23 + n_alias:]
    (x1_ref, xf_ref, rrow_ref, route_ref, ca_ref, cs_ref, ssm_ref, cnt_ref,
     upad, xpad, s_scr, ca_scr, cs_scr, cnt_scr) = refs
    l = pl.program_id(1)
    last = nl - 1
    pa_rows = CONV_A_WIDTH - 1
    ps_rows = SSM_CONV_WIDTH - 1

    @pl.when(jnp.logical_and(pl.program_id(0) == 0, l == 0))
    def _():
        cnt_scr[...] = cnt0_ref[...]

    @pl.when(l == 0)
    def _():
        ca_scr[...] = ca0_ref[...]
        cs_scr[...] = cs0_ref[...]
        for hp in range(SSM_HEADS // 2):
            g, j = divmod(hp, SSM_HEADS // SSM_GROUPS // 2)
            pair = jnp.concatenate([ssm0_ref[2 * hp], ssm0_ref[2 * hp + 1]], axis=0)
            s_scr[g, :, j * LANES:(j + 1) * LANES] = pair.T

    caw = caw_ref[...]
    csw = csw_ref[...]
    a_neg = -jnp.exp(alog_ref[...])
    gam = jnp.concatenate([na_ref[...], ns_ref[...]], axis=1)
    nb = D_SSM + SSM_GROUPS * SSM_STATE
    lane = lax.broadcasted_iota(jnp.int32, (q, LANES), 1)
    lane_f = lane.astype(F32)
    tail_a = ca_scr[...]
    tail_s = cs_scr[...]
    for j in range(tl // q):
        rows = slice(j * q, (j + 1) * q)
        x = x_ref[rows, :]
        xn = (x * lax.rsqrt(jnp.mean(x * x, axis=-1, keepdims=True) + EPS) * nmix_ref[...]).astype(BF16)

        pa = jnp.dot(xn, win_ref[:, 0:COL_A], preferred_element_type=F32)
        a_b, a_c, a_h = pa[:, 0:D_CONV], pa[:, D_CONV:2 * D_CONV], pa[:, 2 * D_CONV:3 * D_CONV]
        u = a_c * a_h
        upad[j, 8 - pa_rows:8, :] = tail_a
        upad[j, 8:8 + q, :] = u
        conv = upad[j, 8 - pa_rows:8 - pa_rows + q, :] * caw[0:1, :]
        for k in range(1, CONV_A_WIDTH):
            conv = conv + upad[j, 8 - pa_rows + k:8 - pa_rows + k + q, :] * caw[k:k + 1, :]
        y_a = a_b * conv
        tail_a = u[q - pa_rows:q, :]

        ps = jnp.dot(xn, win_ref[:, COL_A:D_IN_PAD], preferred_element_type=F32)
        z = ps[:, 0:D_SSM]
        xbc = ps[:, D_SSM:D_SSM + D_XBC]
        xpad[j, 8 - ps_rows:8, :] = tail_s
        xpad[j, 8:8 + q, :] = xbc
        dt = _softplus(ps[:, D_SSM + D_XBC:COL_S] + dtb_ref[...])
        acc = xpad[j, 8 - ps_rows:8 - ps_rows + q, :] * csw[0:1, :]
        for k in range(1, SSM_CONV_WIDTH):
            acc = acc + xpad[j, 8 - ps_rows + k:8 - ps_rows + k + q, :] * csw[k:k + 1, :]
        xc = _silu(acc + csb_ref[...])
        tail_s = xbc[q - ps_rows:q, :]
        y_s = _ssd_chunk(xc[:, 0:D_SSM], xc[:, D_SSM:nb], xc[:, nb:D_XBC], dt, a_neg, z, dsk_ref[...],
                         s_scr, tril_ref[...], e2_ref[...])

        mixed = jnp.concatenate([y_a, y_s], axis=1)
        mean = jnp.dot((mixed * mixed).astype(BF16), g_ref[...], preferred_element_type=F32)
        r_hi, r_mid = _split2(lax.rsqrt(mean + EPS))
        rs_e = jnp.dot(jnp.concatenate([r_hi, r_mid], axis=1), e3_ref[...], preferred_element_type=F32)
        mixed_n = (mixed * rs_e * gam).astype(BF16)
        x1 = x + jnp.dot(mixed_n, wout_ref[...], preferred_element_type=F32)
        x1_ref[rows, :] = x1

        xf = x1 * lax.rsqrt(jnp.mean(x1 * x1, axis=-1, keepdims=True) + EPS) * nffn_ref[...]
        _store_token_tiles(xf_ref, xf, row0=j * q)
        f_hi, f_mid = _split2(xf)
        logits = jnp.dot(jnp.concatenate([f_hi, f_hi, f_mid], axis=1), wr_ref[...],
                         preferred_element_type=F32) + br_ref[...]
        work = logits
        vals, idxs = [], []
        for _ in range(TOP_K):
            m = jnp.max(work, axis=-1, keepdims=True)
            idx = jnp.min(jnp.where(work == m, lane_f, float(LANES)), axis=-1, keepdims=True)
            vals.append(m)
            idxs.append(idx)
            work = jnp.where(lane_f == idx, -jnp.inf, work)
        es = [jnp.exp(v - vals[0]) for v in vals]
        inv = 1.0 / (es[0] + es[1] + es[2] + es[3])
        hots = [(lane_f == idxs[k]).astype(F32) for k in range(TOP_K)]
        hot = hots[0] + hots[1] + hots[2] + hots[3]
        before = jnp.dot(tril_ref[...], hot.astype(BF16), preferred_element_type=F32) - hot + cnt_scr[...]
        cnt_scr[...] = before[q - 1:q, :] + hot[q - 1:q, :]
        route = jnp.zeros((q, LANES), F32)
        for k in range(TOP_K):
            route = jnp.where(lane == k, idxs[k], route)
            route = jnp.where(lane == TOP_K + k, es[k] * inv, route)
            route = jnp.where(lane == 2 * TOP_K + k, jnp.sum(hots[k] * before, axis=-1, keepdims=True), route)
        rrow_ref[rows, :] = route
        route_ref[:, rows] = _transpose_rows(route)[0:ROUTE_ROWS, :]

    ca_scr[...] = tail_a
    cs_scr[...] = tail_s
    cnt_ref[...] = cnt_scr[...]

    @pl.when(l == last)
    def _():
        ca_ref[...] = tail_a
        cs_ref[...] = tail_s
        for hp in range(SSM_HEADS // 2):
            g, j = divmod(hp, SSM_HEADS // SSM_GROUPS // 2)
            pair = s_scr[g, :, j * LANES:(j + 1) * LANES].T
            ssm_ref[2 * hp] = pair[0:SSM_HEAD_DIM, :]
            ssm_ref[2 * hp + 1] = pair[SSM_HEAD_DIM:2 * SSM_HEAD_DIM, :]


def _const_spec(shape):
    nd = len(shape)
    return pl.BlockSpec(shape, lambda b, l: (0,) * nd, pipeline_mode=pl.Buffered(1))


def _mixer_call(x, ca0, cs0, ssm0, cnt0, wts, consts, *, tl, q, t_all, row_off, alias=None):
    b_sz, l_sz, _ = x.shape
    nl = l_sz // tl
    blk_off = row_off // tl
    n_alias = 0 if alias is None else len(alias)
    in_specs = [
        pl.BlockSpec((None, tl, D_MODEL), lambda b, l: (b, l, 0)),
        pl.BlockSpec((None, CONV_A_WIDTH - 1, D_CONV), lambda b, l: (b, 0, 0)),
        pl.BlockSpec((None, SSM_CONV_WIDTH - 1, D_XBC), lambda b, l: (b, 0, 0)),
        pl.BlockSpec((None, SSM_HEADS, SSM_HEAD_DIM, SSM_STATE), lambda b, l: (b, 0, 0, 0)),
        _const_spec(cnt0.shape),
    ] + [_const_spec(w.shape) for w in wts] + [_const_spec(c.shape) for c in consts]
    in_specs += [pl.BlockSpec(memory_space=pl.ANY)] * n_alias
    row_spec = pl.BlockSpec((tl, D_MODEL), lambda b, l: (blk_off + b * nl + l, 0))
    out_specs = [
        row_spec,
        pl.BlockSpec((tl * ROW_TILES, LANES), lambda b, l: (blk_off + b * nl + l, 0)),
        pl.BlockSpec((tl, LANES), lambda b, l: (blk_off + b * nl + l, 0)),
        pl.BlockSpec((None, ROUTE_ROWS, tl), lambda b, l: (b, 0, l)),
        pl.BlockSpec((None, CONV_A_WIDTH - 1, D_CONV), lambda b, l: (b, 0, 0)),
        pl.BlockSpec((None, SSM_CONV_WIDTH - 1, D_XBC), lambda b, l: (b, 0, 0)),
        pl.BlockSpec((None, SSM_HEADS, SSM_HEAD_DIM, SSM_STATE), lambda b, l: (b, 0, 0, 0)),
        pl.BlockSpec((1, LANES), lambda b, l: (0, 0)),
    ]
    out_shape = [
        jax.ShapeDtypeStruct((t_all, D_MODEL), F32),
        jax.ShapeDtypeStruct((t_all * ROW_TILES, LANES), F32),
        jax.ShapeDtypeStruct((t_all, LANES), F32),
        jax.ShapeDtypeStruct((b_sz, ROUTE_ROWS, l_sz), F32),
        jax.ShapeDtypeStruct((b_sz, CONV_A_WIDTH - 1, D_CONV), F32),
        jax.ShapeDtypeStruct((b_sz, SSM_CONV_WIDTH - 1, D_XBC), F32),
        jax.ShapeDtypeStruct((b_sz, SSM_HEADS, SSM_HEAD_DIM, SSM_STATE), F32),
        jax.ShapeDtypeStruct((1, LANES), F32),
    ]
    n_in = 5 + len(wts) + len(consts)
    aliases = {n_in + i: i for i in range(n_alias)}
    args = [x, ca0, cs0, ssm0, cnt0, *wts, *consts] + ([] if alias is None else list(alias))
    return pl.pallas_call(
        functools.partial(_mixer_kernel, tl=tl, q=q, nl=nl, n_alias=n_alias),
        grid=(b_sz, nl),
        in_specs=in_specs,
        out_specs=out_specs,
        out_shape=out_shape,
        scratch_shapes=[
            pltpu.VMEM((tl // q, q + 8, D_CONV), F32),
            pltpu.VMEM((tl // q, q + 8, D_XBC), F32),
            pltpu.VMEM((SSM_GROUPS, SSM_STATE, D_SSM // SSM_GROUPS), F32),
            pltpu.VMEM((CONV_A_WIDTH - 1, D_CONV), F32),
            pltpu.VMEM((SSM_CONV_WIDTH - 1, D_XBC), F32),
            pltpu.VMEM((1, LANES), F32),
        ],
        input_output_aliases=aliases,
        compiler_params=pltpu.CompilerParams(
            dimension_semantics=("arbitrary", "arbitrary"), vmem_limit_bytes=VMEM_LIMIT),
        name="mixer",
    )(*args)


def _mixer_consts(q):
    r = jnp.arange(q)
    tril = (r[:, None] >= r[None, :]).astype(BF16)
    j2 = jnp.arange(2 * LANES) % LANES
    e2 = (j2[:, None] == (jnp.arange(D_SSM) // SSM_HEAD_DIM)[None, :]).astype(BF16)
    c = jnp.arange(D_CONV + D_SSM)
    grp = jnp.where(c < D_CONV, c // (D_CONV // CONV_GROUPS),
                    CONV_GROUPS + (c - D_CONV) // (D_SSM // SSM_GROUPS))
    wgt = jnp.where(c < D_CONV, CONV_GROUPS / D_CONV, SSM_GROUPS / D_SSM)
    g = jnp.where(grp[:, None] == jnp.arange(LANES)[None, :], wgt[:, None], 0.0).astype(BF16)
    e3 = (j2[:, None] == grp[None, :]).astype(BF16)
    return tril, e2, g, e3


def _prep_mixer_weights(norm_mix, w_in, conv_a_w, norm_conv_a, conv_ssm_w, conv_ssm_b, dt_bias, a_log,
                        d_skip, norm_ssm, w_out, norm_ffn, w_router, b_router):
    hpad = LANES - SSM_HEADS
    w_in_p = jnp.pad(w_in[0], ((0, 0), (0, hpad))).astype(BF16)
    wr = jnp.pad(w_router[0], ((0, 0), (0, LANES - N_EXPERTS)))
    wr_hi = wr.astype(BF16)
    wr_mid = (wr - wr_hi.astype(F32)).astype(BF16)
    wr3 = jnp.concatenate([wr_hi, wr_mid, wr_hi], axis=0)
    br = jnp.pad(b_router[0], (0, LANES - N_EXPERTS), constant_values=NEG_BIG)[None]
    return (norm_mix[0][None], w_in_p, conv_a_w[0], norm_conv_a[0][None], conv_ssm_w[0],
            conv_ssm_b[0][None], jnp.pad(dt_bias[0], (0, hpad))[None], jnp.pad(a_log[0], (0, hpad))[None],
            jnp.repeat(d_skip[0], SSM_HEAD_DIM)[None], norm_ssm[0][None], w_out[0].astype(BF16),
            norm_ffn[0][None], wr3, br)


MIX_TL = 512
MIX_Q = 256
COMB_TT = 256


def _routing(top_i, rank, counts):
    t = top_i.shape[1]
    n = t * TOP_K
    assert n % MOE_BLOCK == 0
    nblk = n // MOE_BLOCK
    n_items = nblk + N_EXPERTS
    shift = max(n - 1, 1).bit_length()
    assert N_EXPERTS << shift < 2 ** 31
    ent = (jnp.arange(t, dtype=jnp.int32) * TOP_K)[None, :] + jnp.arange(TOP_K, dtype=jnp.int32)[:, None]
    key = lax.sort((top_i * (1 << shift) + ent).reshape(n))
    slot_tok = (key & ((1 << shift) - 1)) // TOP_K
    group_end = jnp.cumsum(counts)
    group_start = group_end - counts
    pos = rank
    for e in range(N_EXPERTS):
        pos = pos + jnp.where(top_i == e, group_start[e], 0)
    bp = lax.sort(jnp.concatenate([jnp.arange(nblk, dtype=jnp.int32) * MOE_BLOCK, group_start]))
    dup = jnp.concatenate([jnp.zeros((1,), bool), bp[1:] == bp[:-1]])
    bp = lax.sort(jnp.where(jnp.logical_or(dup, bp >= n), n, bp))
    blk = jnp.minimum(bp // MOE_BLOCK, nblk - 1)
    lo = bp - blk * MOE_BLOCK
    exp = jnp.sum((group_end[None, :] <= jnp.minimum(bp, n - 1)[:, None]).astype(jnp.int32), axis=1)
    items = tuple(a.astype(jnp.int32) for a in (blk, exp, lo))
    assert items[0].shape == (n_items,)
    return slot_tok.astype(jnp.int32).reshape(nblk, 1, MOE_BLOCK), pos.astype(jnp.int32), items


def _tile_gather_start(src_hbm, idx_ref, idx_off, n_rows, dst, sem):
    def body(i, carry):
        src_row = pl.multiple_of(idx_ref[0, idx_off + i] * ROW_TILES, ROW_TILES)
        dst_row = pl.multiple_of(i * ROW_TILES, ROW_TILES)
        pltpu.make_async_copy(src_hbm.at[pl.ds(src_row, ROW_TILES), :],
                              dst.at[pl.ds(dst_row, ROW_TILES), :], sem).start()
        return carry
    lax.fori_loop(0, n_rows, body, 0, unroll=8)


def _tile_gather_start_inline(src_hbm, idx_ref, idx_off, n_rows, dst, sem):
    for i in range(n_rows):
        src_row = pl.multiple_of(idx_ref[0, idx_off + i] * ROW_TILES, ROW_TILES)
        pltpu.make_async_copy(src_hbm.at[pl.ds(src_row, ROW_TILES), :],
                              dst.at[pl.ds(i * ROW_TILES, ROW_TILES), :], sem).start(priority=i % 2)


RING = 3


def _expert_kernel(blk_ref, exp_ref, lo_ref, tok_cur, tok_nxt, tok_far, xf_hbm,
                   wg_ref, bg_ref, wu_ref, bu_ref, wd_ref, bd_ref, yb_ref, xbuf, sem, wgb, wub, wdb, acc,
                   *, n_items):
    w = pl.program_id(0)
    lo = lo_ref[w]
    slot = lax.rem(w, RING)
    far = lax.rem(w + RING - 1, RING)

    def wait_rows(s):
        pltpu.make_async_copy(xf_hbm.at[pl.ds(0, MOE_BLOCK * ROW_TILES), :], xbuf.at[s], sem.at[s]).wait()

    @pl.when(w == 0)
    def _():
        _tile_gather_start(xf_hbm, tok_cur, 0, MOE_BLOCK, xbuf.at[0], sem.at[0])
        _tile_gather_start(xf_hbm, tok_nxt, 0, MOE_BLOCK, xbuf.at[1], sem.at[1])

    @pl.when(jnp.logical_or(w == 0, exp_ref[w] != exp_ref[jnp.maximum(w - 1, 0)]))
    def _():
        wgb[...] = wg_ref[...].astype(BF16)
        wub[...] = wu_ref[...].astype(BF16)
        wdb[...] = wd_ref[...].astype(BF16)

    wait_rows(slot)
    xb = _load_token_tiles(xbuf.at[slot], MOE_BLOCK).astype(BF16)
    gate = jnp.dot(xb, wgb[...], preferred_element_type=F32) + bg_ref[...]
    up = jnp.dot(xb, wub[...], preferred_element_type=F32) + bu_ref[...]
    gate = jnp.minimum(gate, SWIGLU_LIMIT)
    up = jnp.clip(up, -SWIGLU_LIMIT, SWIGLU_LIMIT)
    glu = gate * jax.nn.sigmoid(gate * SWIGLU_ALPHA)
    h = ((up + 1.0) * glu).astype(BF16)
    y = jnp.dot(h, wdb[...], preferred_element_type=F32) + bd_ref[...]
    keep = lax.broadcasted_iota(jnp.int32, (MOE_BLOCK, D_MODEL), 0) < lo
    merged = jnp.where(keep, acc[...], y)
    acc[...] = merged
    _store_token_tiles(yb_ref, merged)
    _tile_gather_start_inline(xf_hbm, tok_far, 0, MOE_BLOCK, xbuf.at[far], sem.at[far])

    @pl.when(w == n_items - 1)
    def _():
        for s in range(1, RING):
            wait_rows(lax.rem(w + s, RING))


def _expert_call(slot_tok, items, xf, w_gate, b_gate, w_up, b_up, w_down, b_down):
    nblk = slot_tok.shape[0]
    n_items = items[0].shape[0]
    d, de = w_gate.shape[1], w_gate.shape[2]
    tok_spec = lambda f: pl.BlockSpec((None, 1, MOE_BLOCK), f, memory_space=pltpu.SMEM)
    w_spec = lambda r, c: pl.BlockSpec((None, r, c), lambda w, blk, exp, lo: (exp[w], 0, 0))
    grid_spec = pltpu.PrefetchScalarGridSpec(
        num_scalar_prefetch=3,
        grid=(n_items,),
        in_specs=[
            tok_spec(lambda w, blk, exp, lo: (blk[w], 0, 0)),
            tok_spec(lambda w, blk, exp, lo: (blk[jnp.minimum(w + 1, n_items - 1)], 0, 0)),
            tok_spec(lambda w, blk, exp, lo: (blk[jnp.minimum(w + RING - 1, n_items - 1)], 0, 0)),
            pl.BlockSpec(memory_space=pl.ANY),
            w_spec(d, de), w_spec(1, de), w_spec(d, de), w_spec(1, de), w_spec(de, d), w_spec(1, d),
        ],
        out_specs=pl.BlockSpec((MOE_BLOCK * ROW_TILES, LANES), lambda w, blk, exp, lo: (blk[w], 0)),
        scratch_shapes=[
            pltpu.VMEM((RING, MOE_BLOCK * ROW_TILES, LANES), F32),
            pltpu.SemaphoreType.DMA((RING,)),
            pltpu.VMEM((d, de), BF16), pltpu.VMEM((d, de), BF16), pltpu.VMEM((de, d), BF16),
            pltpu.VMEM((MOE_BLOCK, d), F32),
        ],
    )
    return pl.pallas_call(
        functools.partial(_expert_kernel, n_items=n_items),
        grid_spec=grid_spec,
        out_shape=jax.ShapeDtypeStruct((nblk * MOE_BLOCK * ROW_TILES, LANES), F32),
        compiler_params=pltpu.CompilerParams(dimension_semantics=("arbitrary",), vmem_limit_bytes=VMEM_LIMIT),
        name="experts",
    )(*items, slot_tok, slot_tok, slot_tok, xf, w_gate, b_gate[:, None, :], w_up, b_up[:, None, :],
      w_down, b_down[:, None, :])


def _combine_kernel(pos_cur, pos_nxt, pos_far, x1_ref, rrow_ref, yb_hbm, fn_ref, out_ref, ybuf, sem, *, n):
    i = pl.program_id(0)
    slot = lax.rem(i, RING)
    far = lax.rem(i + RING - 1, RING)
    tt = x1_ref.shape[0]

    def wait_rows(s):
        for k in range(TOP_K):
            pltpu.make_async_copy(yb_hbm.at[pl.ds(0, tt * ROW_TILES), :], ybuf.at[s, k], sem.at[s]).wait()

    @pl.when(i == 0)
    def _():
        for k in range(TOP_K):
            _tile_gather_start(yb_hbm, pos_cur, k * tt, tt, ybuf.at[0, k], sem.at[0])
            _tile_gather_start(yb_hbm, pos_nxt, k * tt, tt, ybuf.at[1, k], sem.at[1])

    wait_rows(slot)
    rr = rrow_ref[...]
    acc = x1_ref[...]
    for k in range(TOP_K):
        acc = acc + rr[:, TOP_K + k:TOP_K + k + 1] * _load_token_tiles(ybuf.at[slot, k], tt)
    out_ref[...] = acc * lax.rsqrt(jnp.mean(acc * acc, axis=-1, keepdims=True) + EPS) * fn_ref[...]
    for k in range(TOP_K):
        _tile_gather_start_inline(yb_hbm, pos_far, k * tt, tt, ybuf.at[far, k], sem.at[far])

    @pl.when(i == n - 1)
    def _():
        for s in range(1, RING):
            wait_rows(lax.rem(i + s, RING))


def _combine_call(pos, x1, rrow, yb, fn, *, row_off, n_rows):
    tt = min(COMB_TT, n_rows)
    n_tiles = n_rows // tt
    off = row_off // tt
    d = x1.shape[1]
    pos = pos[:, row_off:row_off + n_rows].reshape(TOP_K, n_tiles, tt)
    pos = pos.transpose(1, 0, 2).reshape(n_tiles, 1, TOP_K * tt)
    pos_spec = lambda f: pl.BlockSpec((None, 1, tt * TOP_K), f, memory_space=pltpu.SMEM)
    return pl.pallas_call(
        functools.partial(_combine_kernel, n=n_tiles),
        grid=(n_tiles,),
        in_specs=[
            pos_spec(lambda i: (i, 0, 0)),
            pos_spec(lambda i: (jnp.minimum(i + 1, n_tiles - 1), 0, 0)),
            pos_spec(lambda i: (jnp.minimum(i + RING - 1, n_tiles - 1), 0, 0)),
            pl.BlockSpec((tt, d), lambda i: (off + i, 0)),
            pl.BlockSpec((tt, LANES), lambda i: (off + i, 0)),
            pl.BlockSpec(memory_space=pl.ANY),
            pl.BlockSpec((1, d), lambda i: (0, 0)),
        ],
        out_specs=pl.BlockSpec((tt, d), lambda i: (i, 0)),
        out_shape=jax.ShapeDtypeStruct((n_rows, d), F32),
        scratch_shapes=[pltpu.VMEM((RING, TOP_K, tt * ROW_TILES, LANES), F32), pltpu.SemaphoreType.DMA((RING,))],
        compiler_params=pltpu.CompilerParams(dimension_semantics=("arbitrary",), vmem_limit_bytes=VMEM_LIMIT),
        name="combine",
    )(pos, pos, pos, x1, rrow, yb, fn)


def kernel(x_prompt, x_sample, cache_conv_a, cache_conv_ssm, state_ssm, norm_mix, w_in, conv_a_w, norm_conv_a, conv_ssm_w, conv_ssm_b, dt_bias, a_log, d_skip, norm_ssm, w_out, norm_ffn, w_router, b_router, w_gate, b_gate, w_up, b_up, w_down, b_down, final_norm):
    bp, lp, _ = x_prompt.shape
    bs, ls, _ = x_sample.shape
    t_p, t_s = bp * lp, bs * ls
    t_all = t_p + t_s
    wts = _prep_mixer_weights(norm_mix, w_in, conv_a_w, norm_conv_a, conv_ssm_w, conv_ssm_b, dt_bias, a_log,
                              d_skip, norm_ssm, w_out, norm_ffn, w_router, b_router)
    zeros = lambda *s: jnp.zeros(s, F32)
    x1, xf, rrow, route_p, ca_p, cs_p, st_p, cnt = _mixer_call(
        x_prompt, zeros(bp, CONV_A_WIDTH - 1, D_CONV), zeros(bp, SSM_CONV_WIDTH - 1, D_XBC),
        zeros(bp, SSM_HEADS, SSM_HEAD_DIM, SSM_STATE), zeros(1, LANES), wts, _mixer_consts(MIX_Q),
        tl=MIX_TL, q=MIX_Q, t_all=t_all, row_off=0)
    x1, xf, rrow, route_s, ca_s, cs_s, st_s, cnt = _mixer_call(
        x_sample, cache_conv_a[0], cache_conv_ssm[0], state_ssm[0], cnt, wts,
        _mixer_consts(ls), tl=ls, q=ls, t_all=t_all, row_off=t_p, alias=(x1, xf, rrow))

    def rows(r0):
        return jnp.concatenate([route_p[:, r0:r0 + TOP_K, :].transpose(1, 0, 2).reshape(TOP_K, t_p),
                                route_s[:, r0:r0 + TOP_K, :].transpose(1, 0, 2).reshape(TOP_K, t_s)],
                               axis=1).astype(jnp.int32)
    slot_tok, pos, items = _routing(rows(0), rows(2 * TOP_K), cnt[0, :N_EXPERTS].astype(jnp.int32))

    yb = _expert_call(slot_tok, items, xf, w_gate[0], b_gate[0], w_up[0], b_up[0], w_down[0], b_down[0])
    fn = final_norm[None]
    y_p = _combine_call(pos, x1, rrow, yb, fn, row_off=0, n_rows=t_p)
    y_s = _combine_call(pos, x1, rrow, yb, fn, row_off=t_p, n_rows=t_s)
    return (y_p.reshape(bp, lp, D_MODEL), y_s.reshape(bs, ls, D_MODEL),
            ca_p[None], cs_p[None], st_p[None], ca_s[None], cs_s[None], st_s[None])
```
